```python
import jax, jax.numpy as jnp
from jax import lax
import numpy as np

D_MODEL = 2048
BATCH = 4
SEQ = 2048
DEPTH = 2
DEC_BATCH = 128
DEC_SEQ = 8
PAST_LEN = 2048
PAGE_SIZE = 128

N_A_LAYERS = DEPTH // 2
N_B_LAYERS = DEPTH - N_A_LAYERS
CHUNK = 128
A_WIDTH = D_MODEL
A_GROUPS = 8
A_GROUP_DIM = A_WIDTH // A_GROUPS
HEAD_DIM = 128
N_KV_HEADS = 8
KV_WIDTH = N_KV_HEADS * HEAD_DIM
DILATED_GROUPS = ((128, 1), (512, 4), (2048, 16))
N_DGROUPS = len(DILATED_GROUPS)
MAX_WINDOW = 2048
ATTN_BLOCK = 128
ROPE_THETA = 10000.0
N_KEYS = 128
N_EXPERTS = N_KEYS * N_KEYS
PEER_HEADS = 8
PEER_KEY_DIM = 256
PEER_HALF = PEER_KEY_DIM // 2
PEER_TOPK = 16
PEER_BLOCK = 128
PLE_DIM = 256
RMS_EPS = 1e-6
LN_EPS = 1e-5

kernel_name = 'hybrid_gmlp_dilated_peer_step'


def _rms_norm(x, g):
    xf = x.astype(jnp.float32)
    y = xf * lax.rsqrt(jnp.mean(xf * xf, axis=-1, keepdims=True) + RMS_EPS)
    return (y * g.astype(jnp.float32)).astype(x.dtype)


def _layer_norm(x, g, b):
    xf = x.astype(jnp.float32)
    mu = jnp.mean(xf, axis=-1, keepdims=True)
    var = jnp.mean(jnp.square(xf - mu), axis=-1, keepdims=True)
    y = (xf - mu) * lax.rsqrt(var + LN_EPS)
    return (y * g.astype(jnp.float32) + b.astype(jnp.float32)).astype(x.dtype)


def _rope(x, pos):
    half = HEAD_DIM // 2
    inv_freq = ROPE_THETA ** (-jnp.arange(half, dtype=jnp.float32) / half)
    ang = pos.astype(jnp.float32)[:, None] * inv_freq[None, :]
    cos = jnp.cos(ang)[None, :, None, :]
    sin = jnp.sin(ang)[None, :, None, :]
    xf = x.astype(jnp.float32)
    x1, x2 = xf[..., :half], xf[..., half:]
    return jnp.concatenate([x1 * cos - x2 * sin, x2 * cos + x1 * sin], axis=-1).astype(x.dtype)


def _spatial_mix(vc, w_s, b_s):
    n = vc.shape[-3]
    w = jnp.where(jnp.tril(jnp.ones((n, n), dtype=bool)), w_s[:, :n, :n], 0.0)
    return jnp.einsum('gts,...sgd->...tgd', w, vc) + b_s[:, :n].T[:, :, None]


def _gmlp_mixer(hn, w_in, ln_g, ln_b, w_s, b_s, w_out, full_chunks):
    B, S, _ = hn.shape
    z = jax.nn.gelu(hn @ w_in)
    u, v = jnp.split(z, 2, axis=-1)
    v = _layer_norm(v, ln_g, ln_b)
    if full_chunks:
        vc = v.reshape(B, S // CHUNK, CHUNK, A_GROUPS, A_GROUP_DIM)
        rows = v[:, ((S - 1) // CHUNK) * CHUNK:]
    else:
        vc = v.reshape(B, S, A_GROUPS, A_GROUP_DIM)
        rows = v
    mixed = _spatial_mix(vc, w_s, b_s).reshape(B, S, A_WIDTH)
    return (u * mixed) @ w_out, rows


def _dilated_attn_prompt(q, k, v, window, dil):
    B, S, H, Dh = q.shape
    L = S // dil
    n_back = window // dil
    nblk = -(-L // ATTN_BLOCK)
    Lp = nblk * ATTN_BLOCK

    def to_res(x):
        x = x.reshape(B, L, dil, H, Dh).transpose(0, 2, 1, 3, 4)
        return jnp.pad(x, ((0, 0), (0, 0), (0, Lp - L), (0, 0), (0, 0)))

    def band(x):
        xp = jnp.pad(x, ((0, 0), (0, 0), (ATTN_BLOCK, 0), (0, 0), (0, 0)))
        xp = xp.reshape(B, dil, nblk + 1, ATTN_BLOCK, H, Dh)
        return jnp.concatenate([xp[:, :, :-1], xp[:, :, 1:]], axis=3)

    qb = to_res(q).reshape(B, dil, nblk, ATTN_BLOCK, H, Dh)
    kb = band(to_res(k))
    vb = band(to_res(v))
    qi = jnp.arange(nblk)[:, None, None] * ATTN_BLOCK + jnp.arange(ATTN_BLOCK)[None, :, None]
    ki = (jnp.arange(nblk)[:, None, None] * ATTN_BLOCK - ATTN_BLOCK
          + jnp.arange(2 * ATTN_BLOCK)[None, None, :])
    diff = qi - ki
    mask = (diff >= 0) & (diff <= n_back) & (ki >= 0)
    s = jnp.einsum('brnqhe,brnkhe->brhnqk', qb, kb).astype(jnp.float32) * (HEAD_DIM ** -0.5)
    s = jnp.where(mask[None, None, None], s, -jnp.inf)
    lse = jax.nn.logsumexp(s, axis=-1)
    p = jnp.exp(s - lse[..., None])
    o = jnp.einsum('brhnqk,brnkhe->brnqhe', p.astype(v.dtype), vb)
    o = o.reshape(B, dil, Lp, H, Dh)[:, :, :L].transpose(0, 2, 1, 3, 4).reshape(B, S, H, Dh)
    lse = lse.transpose(0, 1, 3, 4, 2).reshape(B, dil, Lp, H)[:, :, :L]
    lse = lse.transpose(0, 2, 1, 3).reshape(B, S, H)
    return o, lse


def _dilated_attn_sample(q, k_new, v_new, cache_kv, window, dil):
    DB, DS, H, Dh = q.shape
    w_buf = cache_kv.shape[1]
    n_back = window // dil
    scale = HEAD_DIM ** -0.5
    j = jnp.arange(DS)[:, None]
    off = j - jnp.arange(1, n_back + 1)[None, :] * dil
    past_ok = (off < 0) & (off >= -w_buf)
    idx = jnp.clip(w_buf + off, 0, w_buf - 1)
    k_past = cache_kv[:, idx, 0]
    v_past = cache_kv[:, idx, 1]
    s_past = jnp.einsum('bqhe,bqkhe->bhqk', q, k_past).astype(jnp.float32) * scale
    s_past = jnp.where(past_ok[None, None], s_past, -jnp.inf)
    diff = j - jnp.arange(DS)[None, :]
    new_ok = (diff >= 0) & (diff % dil == 0) & (diff <= window)
    s_new = jnp.einsum('bqhe,bkhe->bhqk', q, k_new).astype(jnp.float32) * scale
    s_new = jnp.where(new_ok[None, None], s_new, -jnp.inf)
    s = jnp.concatenate([s_past, s_new], axis=-1)
    lse = jax.nn.logsumexp(s, axis=-1)
    p = jnp.exp(s - lse[..., None]).astype(v_new.dtype)
    o = (jnp.einsum('bhqk,bqkhe->bqhe', p[..., :n_back], v_past)
         + jnp.einsum('bhqk,bkhe->bqhe', p[..., n_back:], v_new))
    return o, lse.transpose(0, 2, 1)


def _dilated_mixer(hn, k, v, pos, w_q, w_o, cache_kv):
    B, S, _ = hn.shape
    q = _rope((hn @ w_q).reshape(B, S, N_DGROUPS * N_KV_HEADS, HEAD_DIM), pos)
    q = q.reshape(B, S, N_DGROUPS, N_KV_HEADS, HEAD_DIM)
    outs, lses = [], []
    for g, (window, dil) in enumerate(DILATED_GROUPS):
        if cache_kv is None:
            o, l = _dilated_attn_prompt(q[:, :, g], k, v, window, dil)
        else:
            o, l = _dilated_attn_sample(q[:, :, g], k, v, cache_kv, window, dil)
        outs.append(o)
        lses.append(l)
    w = jax.nn.softmax(jnp.stack(lses, axis=0), axis=0)
    o = jnp.einsum('gbsh,gbshe->bshe', w.astype(outs[0].dtype), jnp.stack(outs, axis=0))
    return o.reshape(B, S, KV_WIDTH) @ w_o


def _peer_ffn(xn, w_pq, sub_keys, u_tab, v_tab):
    B, S, D = xn.shape
    n_tok = B * S
    n_blk = -(-n_tok // PEER_BLOCK)
    xt = jnp.pad(xn.reshape(n_tok, D), ((0, n_blk * PEER_BLOCK - n_tok), (0, 0)))
    xt = xt.reshape(n_blk, PEER_BLOCK, D)
    n_cand = PEER_TOPK * PEER_TOPK

    def block(xb):
        q = (xb @ w_pq).reshape(PEER_BLOCK, PEER_HEADS, 2, PEER_HALF)
        s = jnp.einsum('thcd,hcnd->thcn', q, sub_keys).astype(jnp.float32)
        s1, i1 = lax.top_k(s[:, :, 0], PEER_TOPK)
        s2, i2 = lax.top_k(s[:, :, 1], PEER_TOPK)
        cand = (s1[..., :, None] + s2[..., None, :]).reshape(PEER_BLOCK, PEER_HEADS, n_cand)
        cand_idx = (i1[..., :, None] * N_KEYS + i2[..., None, :]).reshape(PEER_BLOCK, PEER_HEADS, n_cand)
        top_s, top_pos = lax.top_k(cand, PEER_TOPK)
        expert = jnp.take_along_axis(cand_idx, top_pos, axis=-1)
        gate = jax.nn.softmax(top_s, axis=-1)
        act = jax.nn.gelu(jnp.einsum('td,thkd->thk', xb, u_tab[expert]).astype(jnp.float32))
        return jnp.einsum('thk,thkd->td', (gate * act).astype(xb.dtype), v_tab[expert])

    y = lax.map(block, xt)
    return y.reshape(n_blk * PEER_BLOCK, D)[:n_tok].reshape(B, S, D)


def _per_layer_embed(h, p_i, g, w_gate, w_proj):
    gate = jax.nn.sigmoid(_rms_norm(h, g) @ w_gate)
    return h + gate * (p_i @ w_proj)


def _trunk(x, p, pos, cache_kv, g_mix, a_w_in, a_ln_g, a_ln_b, a_w_s, a_b_s, a_w_out,
           g_kv, w_kv, b_w_q, b_w_o, g_ffn, peer_w_q, peer_keys, peer_u, peer_v,
           g_ple, ple_w_gate, ple_w_proj, g_final):
    B, S, _ = x.shape
    h = x
    k_sh = None
    v_sh = None
    a_rows = []
    for i in range(DEPTH):
        hn = _rms_norm(h, g_mix[i])
        if i < N_A_LAYERS:
            mix, rows = _gmlp_mixer(hn, a_w_in[i], a_ln_g[i], a_ln_b[i], a_w_s[i], a_b_s[i],
                                    a_w_out[i], cache_kv is None)
            a_rows.append(rows)
        else:
            jb = i - N_A_LAYERS
            mix = _dilated_mixer(hn, k_sh, v_sh, pos, b_w_q[jb], b_w_o[jb], cache_kv)
        h = h + mix
        h = h + _peer_ffn(_rms_norm(h, g_ffn[i]), peer_w_q[i], peer_keys[i], peer_u[i], peer_v[i])
        h = _per_layer_embed(h, p[i], g_ple[i], ple_w_gate[i], ple_w_proj[i])
        if i == N_A_LAYERS - 1:
            kv = (_rms_norm(h, g_kv) @ w_kv).reshape(B, S, 2, N_KV_HEADS, HEAD_DIM)
            k_sh = _rope(kv[:, :, 0], pos)
            v_sh = kv[:, :, 1]
    y = _rms_norm(h, g_final)
    return y, jnp.stack(a_rows, axis=0), jnp.stack([k_sh, v_sh], axis=2)


def setup_inputs(seed: int = 0) -> dict:
    key = jax.random.key(seed)
    ks = iter(jax.random.split(key, 40))

    def nrm(shape, scale):
        return scale * jax.random.normal(next(ks), shape, jnp.float32)

    def gain(shape):
        return 1.0 + nrm(shape, 0.01)

    w_buf = min(MAX_WINDOW, PAST_LEN)
    return {
        'x_prompt': nrm((BATCH, SEQ, D_MODEL), 1.0),
        'x_sample': nrm((DEC_BATCH, DEC_SEQ, D_MODEL), 1.0),
        'p_prompt': nrm((DEPTH, BATCH, SEQ, PLE_DIM), 1.0),
        'p_sample': nrm((DEPTH, DEC_BATCH, DEC_SEQ, PLE_DIM), 1.0),
        'cache_kv': nrm((DEC_BATCH, w_buf, 2, N_KV_HEADS, HEAD_DIM), 1.0),
        'g_mix': gain((DEPTH, D_MODEL)),
        'a_w_in': nrm((N_A_LAYERS, D_MODEL, 2 * A_WIDTH), D_MODEL ** -0.5),
        'a_ln_g': gain((N_A_LAYERS, A_WIDTH)),
        'a_ln_b': nrm((N_A_LAYERS, A_WIDTH), 0.01),
        'a_w_s': nrm((N_A_LAYERS, A_GROUPS, CHUNK, CHUNK), CHUNK ** -0.5),
        'a_b_s': gain((N_A_LAYERS, A_GROUPS, CHUNK)),
        'a_w_out': nrm((N_A_LAYERS, A_WIDTH, D_MODEL), A_WIDTH ** -0.5),
        'g_kv': gain((D_MODEL,)),
        'w_kv': nrm((D_MODEL, 2 * KV_WIDTH), D_MODEL ** -0.5),
        'b_w_q': nrm((N_B_LAYERS, D_MODEL, N_DGROUPS * KV_WIDTH), D_MODEL ** -0.5),
        'b_w_o': nrm((N_B_LAYERS, KV_WIDTH, D_MODEL), KV_WIDTH ** -0.5),
        'g_ffn': gain((DEPTH, D_MODEL)),
        'peer_w_q': nrm((DEPTH, D_MODEL, PEER_HEADS * PEER_KEY_DIM), D_MODEL ** -0.5),
        'peer_keys': nrm((DEPTH, PEER_HEADS, 2, N_KEYS, PEER_HALF), PEER_HALF ** -0.5),
        'peer_u': nrm((DEPTH, N_EXPERTS, D_MODEL), D_MODEL ** -0.5),
        'peer_v': nrm((DEPTH, N_EXPERTS, D_MODEL), PEER_HEADS ** -0.5),
        'g_ple': gain((DEPTH, D_MODEL)),
        'ple_w_gate': nrm((DEPTH, D_MODEL, D_MODEL), D_MODEL ** -0.5),
        'ple_w_proj': nrm((DEPTH, PLE_DIM, D_MODEL), PLE_DIM ** -0.5),
        'g_final': gain((D_MODEL,)),
    }


def reference(x_prompt, x_sample, p_prompt, p_sample, cache_kv, g_mix, a_w_in, a_ln_g, a_ln_b,
              a_w_s, a_b_s, a_w_out, g_kv, w_kv, b_w_q, b_w_o, g_ffn, peer_w_q, peer_keys,
              peer_u, peer_v, g_ple, ple_w_gate, ple_w_proj, g_final):
    seq = x_prompt.shape[1]
    dec_seq = x_sample.shape[1]
    pos_prompt = jnp.arange(seq, dtype=jnp.int32)
    pos_sample = PAST_LEN + jnp.arange(dec_seq, dtype=jnp.int32)
    y_prompt, a_v_prompt, kv_prompt = _trunk(
        x_prompt, p_prompt, pos_prompt, None, g_mix, a_w_in, a_ln_g, a_ln_b, a_w_s, a_b_s,
        a_w_out, g_kv, w_kv, b_w_q, b_w_o, g_ffn, peer_w_q, peer_keys, peer_u, peer_v,
        g_ple, ple_w_gate, ple_w_proj, g_final)
    y_sample, a_v_sample, kv_sample = _trunk(
        x_sample, p_sample, pos_sample, cache_kv, g_mix, a_w_in, a_ln_g, a_ln_b, a_w_s, a_b_s,
        a_w_out, g_kv, w_kv, b_w_q, b_w_o, g_ffn, peer_w_q, peer_keys, peer_u, peer_v,
        g_ple, ple_w_gate, ple_w_proj, g_final)
    kv_prompt = kv_prompt[:, seq - min(MAX_WINDOW, seq):]
    return (y_prompt, y_sample, a_v_prompt, a_v_sample, kv_prompt, kv_sample)
```

```python
import functools

import jax
import jax.numpy as jnp
from jax import lax
from jax.experimental import pallas as pl
from jax.experimental.pallas import tpu as pltpu

F32 = jnp.float32
BF16 = jnp.bfloat16

D_MODEL = 2048
BATCH = 4
SEQ = 2048
DEC_BATCH = 128
DEC_SEQ = 8
PAST_LEN = 2048
N_PROMPT = BATCH * SEQ
N_SAMPLE = DEC_BATCH * DEC_SEQ
N_TOK = N_PROMPT + N_SAMPLE

CHUNK = 128
A_WIDTH = D_MODEL
A_GROUPS = 8
A_GROUP_DIM = A_WIDTH // A_GROUPS

HEAD_DIM = 128
N_KV_HEADS = 8
KV_WIDTH = N_KV_HEADS * HEAD_DIM
DILATED_GROUPS = ((128, 1), (512, 4), (2048, 16))
N_DGROUPS = len(DILATED_GROUPS)
Q_WIDTH = N_DGROUPS * KV_WIDTH
ATTN_BLOCK = 128
ROPE_THETA = 10000.0
ATTN_SCALE = HEAD_DIM ** -0.5

N_KEYS = 128
N_EXPERTS = N_KEYS * N_KEYS
PEER_HEADS = 8
PEER_HALF = 128
PEER_TOPK = 16
PLE_DIM = 256
RMS_EPS = 1e-6
LN_EPS = 1e-5

V7X_VMEM_BYTES = 64 * 1024 * 1024
VMEM_LIMIT = V7X_VMEM_BYTES - 8 * 1024 * 1024
LANES = 128

ROW_TILE = 256
PEER_TOK_TILE = 512
PEER_EXP_TILE = 1024
PEER_I1_PER_STEP = PEER_EXP_TILE // N_KEYS

NT_DIMS = (((1,), (1,)), ((), ()))


def _params(*sem):
    return pltpu.CompilerParams(dimension_semantics=sem, vmem_limit_bytes=VMEM_LIMIT)


def _resident(shape):
    return pl.BlockSpec(shape, lambda *_: (0,) * len(shape), pipeline_mode=pl.Buffered(1))


def _rows(width, tile=ROW_TILE):
    return pl.BlockSpec((tile, width), lambda i: (i, 0))


def _rms(x, g):
    return x * lax.rsqrt(jnp.mean(x * x, axis=-1, keepdims=True) + RMS_EPS) * g


def _gmlp_in_kernel(h_ref, g_ref, w_ref, lng_ref, lnb_ref, u_ref, v_ref):
    xn = _rms(h_ref[...], g_ref[...]).astype(BF16)
    u_ref[...] = jax.nn.gelu(jnp.dot(xn, w_ref[:, :A_WIDTH], preferred_element_type=F32))
    v = jax.nn.gelu(jnp.dot(xn, w_ref[:, A_WIDTH:], preferred_element_type=F32))
    mu = jnp.mean(v, axis=-1, keepdims=True)
    vc = v - mu
    var = jnp.mean(vc * vc, axis=-1, keepdims=True)
    v_ref[...] = vc * lax.rsqrt(var + LN_EPS) * lng_ref[...] + lnb_ref[...]


def _gmlp_in(h, g, w_in, ln_g, ln_b):
    return pl.pallas_call(
        _gmlp_in_kernel,
        grid=(N_TOK // ROW_TILE,),
        in_specs=[_rows(D_MODEL), _resident((1, D_MODEL)), _resident((D_MODEL, 2 * A_WIDTH)),
                  _resident((1, A_WIDTH)), _resident((1, A_WIDTH))],
        out_specs=[_rows(A_WIDTH), _rows(A_WIDTH)],
        out_shape=[jax.ShapeDtypeStruct((N_TOK, A_WIDTH), F32)] * 2,
        compiler_params=_params("parallel"),
    )(h, g, w_in, ln_g, ln_b)


def _gmlp_out_kernel(u_ref, v_ref, h_ref, wmix_ref, bias_ref, wout_ref, o_ref, um_ref):
    for c in range(ROW_TILE // CHUNK):
        rows = slice(c * CHUNK, (c + 1) * CHUNK)
        for g in range(A_GROUPS):
            cols = slice(g * A_GROUP_DIM, (g + 1) * A_GROUP_DIM)
            mixed = jnp.dot(wmix_ref[0, g], v_ref[rows, cols].astype(BF16),
                            preferred_element_type=F32) + bias_ref[0, :, cols]
            um_ref[rows, cols] = (u_ref[rows, cols] * mixed).astype(BF16)
    o_ref[...] = h_ref[...] + jnp.dot(um_ref[...], wout_ref[...], preferred_element_type=F32)


def _gmlp_out(u, v, h, wmix, bias, w_out):
    prompt_tiles = N_PROMPT // ROW_TILE
    return pl.pallas_call(
        _gmlp_out_kernel,
        grid=(N_TOK // ROW_TILE,),
        in_specs=[_rows(A_WIDTH), _rows(A_WIDTH), _rows(D_MODEL),
                  pl.BlockSpec((1, A_GROUPS, CHUNK, CHUNK), lambda i: (i // prompt_tiles, 0, 0, 0)),
                  pl.BlockSpec((1, CHUNK, A_WIDTH), lambda i: (i // prompt_tiles, 0, 0)),
                  _resident((A_WIDTH, D_MODEL))],
        out_specs=_rows(D_MODEL),
        out_shape=jax.ShapeDtypeStruct((N_TOK, D_MODEL), F32),
        scratch_shapes=[pltpu.VMEM((ROW_TILE, A_WIDTH), BF16)],
        compiler_params=_params("parallel"),
    )(u, v, h, wmix, bias, w_out)


def _top_values(s, top_ref, base):
    work = s
    for k in range(PEER_TOPK):
        m = jnp.max(work, axis=0, keepdims=True)
        top_ref[base + k:base + k + 1, :] = m
        work = jnp.where(work == m, -jnp.inf, work)


def _peer_route_kernel(h_ref, g_ref, wpq_ref, keys_ref, xn_ref, a1_ref, g1_ref, s2_ref, g2_ref, top_ref):
    xn = _rms(h_ref[...], g_ref[...]).astype(BF16)
    xn_ref[...] = xn
    q = jnp.dot(xn, wpq_ref[...], preferred_element_type=F32).astype(BF16)
    for hd in range(PEER_HEADS):
        c0 = hd * 2 * PEER_HALF
        s1 = lax.dot_general(keys_ref[hd, 0], q[:, c0:c0 + PEER_HALF], NT_DIMS,
                             preferred_element_type=F32)
        s2 = lax.dot_general(keys_ref[hd, 1], q[:, c0 + PEER_HALF:c0 + 2 * PEER_HALF], NT_DIMS,
                             preferred_element_type=F32)
        _top_values(s1, top_ref, 0)
        _top_values(s2, top_ref, PEER_TOPK)
        m1 = top_ref[0:1, :]
        m2 = top_ref[PEER_TOPK:PEER_TOPK + 1, :]
        second_lo = top_ref[PEER_TOPK:PEER_TOPK + 8, :]
        cands = [m1 + second_lo, m1 + top_ref[PEER_TOPK + 8:2 * PEER_TOPK, :]]
        for a in range(1, PEER_TOPK):
            cands.append(top_ref[a:a + 1, :] + second_lo)
        top = m1 + m2
        zsum = jnp.zeros_like(top)
        thr = top
        for k in range(PEER_TOPK):
            m = jnp.max(functools.reduce(jnp.maximum, cands), axis=0, keepdims=True)
            zsum = zsum + jnp.exp(m - top)
            thr = m
            if k + 1 < PEER_TOPK:
                cands = [jnp.where(c == m, -jnp.inf, c) for c in cands]
        a1_ref[hd] = thr - s1
        g1_ref[hd] = jnp.exp(s1 - m1) / zsum
        s2_ref[hd] = s2
        g2_ref[hd] = jnp.exp(s2 - m2)


def _peer_route(h, g, w_pq, keys):
    route_spec = pl.BlockSpec((PEER_HEADS, N_KEYS, ROW_TILE), lambda i: (0, 0, i))
    route_shape = jax.ShapeDtypeStruct((PEER_HEADS, N_KEYS, N_TOK), F32)
    return pl.pallas_call(
        _peer_route_kernel,
        grid=(N_TOK // ROW_TILE,),
        in_specs=[_rows(D_MODEL), _resident((1, D_MODEL)), _resident((D_MODEL, D_MODEL)),
                  _resident((PEER_HEADS, 2, N_KEYS, PEER_HALF))],
        out_specs=[_rows(D_MODEL)] + [route_spec] * 4,
        out_shape=[jax.ShapeDtypeStruct((N_TOK, D_MODEL), BF16)] + [route_shape] * 4,
        scratch_shapes=[pltpu.VMEM((2 * PEER_TOPK, ROW_TILE), F32)],
        compiler_params=_params("parallel"),
    )(h, g, w_pq, keys)


def _peer_dense_kernel(x_ref, u_ref, vt_ref, a1_ref, g1_ref, s2_ref, g2_ref, y_ref, acc_ref, act_ref, w_ref):
    j = pl.program_id(1)

    @pl.when(j == 0)
    def _():
        acc_ref[...] = jnp.zeros_like(acc_ref)

    act_ref[...] = jax.nn.gelu(lax.dot_general(u_ref[...], x_ref[...], NT_DIMS, preferred_element_type=F32))
    for ii in range(PEER_I1_PER_STEP):
        rs = slice(ii * N_KEYS, (ii + 1) * N_KEYS)
        for lt in range(PEER_TOK_TILE // LANES):
            ls = slice(lt * LANES, (lt + 1) * LANES)
            coef = None
            for hd in range(PEER_HEADS):
                a = a1_ref[hd, ii:ii + 1, ls]
                gg = g1_ref[hd, ii:ii + 1, ls]
                term = jnp.where(s2_ref[hd, :, ls] >= a, g2_ref[hd, :, ls], 0.0) * gg
                coef = term if coef is None else coef + term
            w_ref[rs, ls] = (act_ref[rs, ls] * coef).astype(BF16)
    acc_ref[...] += jnp.dot(vt_ref[...], w_ref[...], preferred_element_type=F32)

    @pl.when(j == pl.num_programs(1) - 1)
    def _():
        y_ref[...] = acc_ref[...].T


def _peer_dense(xn, u_bf, vt_bf, a1, g1, s2, g2):
    first_spec = pl.BlockSpec((PEER_HEADS, PEER_I1_PER_STEP, PEER_TOK_TILE), lambda i, j: (0, j, i))
    second_spec = pl.BlockSpec((PEER_HEADS, N_KEYS, PEER_TOK_TILE), lambda i, j: (0, 0, i))
    return pl.pallas_call(
        _peer_dense_kernel,
        grid=(N_TOK // PEER_TOK_TILE, N_EXPERTS // PEER_EXP_TILE),
        in_specs=[pl.BlockSpec((PEER_TOK_TILE, D_MODEL), lambda i, j: (i, 0)),
                  pl.BlockSpec((PEER_EXP_TILE, D_MODEL), lambda i, j: (j, 0)),
                  pl.BlockSpec((D_MODEL, PEER_EXP_TILE), lambda i, j: (0, j)),
                  first_spec, first_spec, second_spec, second_spec],
        out_specs=pl.BlockSpec((PEER_TOK_TILE, D_MODEL), lambda i, j: (i, 0)),
        out_shape=jax.ShapeDtypeStruct((N_TOK, D_MODEL), F32),
        scratch_shapes=[pltpu.VMEM((D_MODEL, PEER_TOK_TILE), F32),
                        pltpu.VMEM((PEER_EXP_TILE, PEER_TOK_TILE), F32),
                        pltpu.VMEM((PEER_EXP_TILE, PEER_TOK_TILE), BF16)],
        compiler_params=_params("parallel", "arbitrary"),
    )(xn, u_bf, vt_bf, a1, g1, s2, g2)


def _peer(h, g, w_pq, keys, u_tab, v_tab):
    xn, a1, g1, s2, g2 = _peer_route(h, g, w_pq.astype(BF16), keys.astype(BF16))
    return _peer_dense(xn, u_tab.astype(BF16), v_tab.T.astype(BF16), a1, g1, s2, g2)


def _ple_kernel(h_ref, y_ref, p_ref, g_ref, wg_ref, wp_ref, *rest, final):
    h = h_ref[...] + y_ref[...]
    gate = jax.nn.sigmoid(jnp.dot(_rms(h, g_ref[...]).astype(BF16), wg_ref[...], preferred_element_type=F32))
    out = h + gate * jnp.dot(p_ref[...].astype(BF16), wp_ref[...], preferred_element_type=F32)
    if final:
        gf_ref, o_ref, yf_ref = rest
        yf_ref[...] = _rms(out, gf_ref[...])
    else:
        (o_ref,) = rest
    o_ref[...] = out


def _ple(h, y, p, g, w_gate, w_proj, g_final=None):
    final = g_final is not None
    in_specs = [_rows(D_MODEL), _rows(D_MODEL), _rows(PLE_DIM), _resident((1, D_MODEL)),
                _resident((D_MODEL, D_MODEL)), _resident((PLE_DIM, D_MODEL))]
    args = [h, y, p, g, w_gate, w_proj]
    n_out = 1
    if final:
        in_specs.append(_resident((1, D_MODEL)))
        args.append(g_final)
        n_out = 2
    return pl.pallas_call(
        functools.partial(_ple_kernel, final=final),
        grid=(N_TOK // ROW_TILE,),
        in_specs=in_specs,
        out_specs=[_rows(D_MODEL)] * n_out,
        out_shape=[jax.ShapeDtypeStruct((N_TOK, D_MODEL), F32)] * n_out,
        compiler_params=_params("parallel"),
    )(*args)


def _proj_rope_kernel(h_ref, g_ref, w_ref, cos_ref, sin_ref, o_ref, *, n_rope_heads, width):
    xn = _rms(h_ref[...], g_ref[...]).astype(BF16)
    cos = cos_ref[...]
    sin = sin_ref[...]
    for c0 in range(0, width, KV_WIDTH):
        z = jnp.dot(xn, w_ref[:, c0:c0 + KV_WIDTH], preferred_element_type=F32)
        for hd in range(N_KV_HEADS):
            cs = slice(hd * HEAD_DIM, (hd + 1) * HEAD_DIM)
            zh = z[:, cs]
            if c0 // HEAD_DIM + hd < n_rope_heads:
                zh = zh * cos + pltpu.roll(zh, HEAD_DIM // 2, 1) * sin
            o_ref[:, c0 + hd * HEAD_DIM:c0 + (hd + 1) * HEAD_DIM] = zh


def _proj_rope(h, g, w, cos, sin, n_rope_heads):
    width = w.shape[1]
    return pl.pallas_call(
        functools.partial(_proj_rope_kernel, n_rope_heads=n_rope_heads, width=width),
        grid=(N_TOK // ROW_TILE,),
        in_specs=[_rows(D_MODEL), _resident((1, D_MODEL)), _resident((D_MODEL, width)),
                  _rows(HEAD_DIM), _rows(HEAD_DIM)],
        out_specs=_rows(width),
        out_shape=jax.ShapeDtypeStruct((N_TOK, width), F32),
        compiler_params=_params("parallel"),
    )(h, g, w, cos, sin)


def _softmax_parts(scores):
    m = functools.reduce(jnp.maximum, [jnp.max(s, axis=-1, keepdims=True) for s in scores])
    probs = [jnp.exp(s - m) for s in scores]
    z = functools.reduce(jnp.add, [jnp.sum(p, axis=-1, keepdims=True) for p in probs])
    return probs, z, m + jnp.log(z)


def _attn_prompt_kernel(q_ref, kp_ref, kc_ref, vp_ref, vc_ref, o_ref, lse_ref):
    blk = pl.program_id(2)
    iq = lax.broadcasted_iota(jnp.int32, (ATTN_BLOCK, 2 * ATTN_BLOCK), 0)
    ik = lax.broadcasted_iota(jnp.int32, (ATTN_BLOCK, 2 * ATTN_BLOCK), 1)
    diff = ATTN_BLOCK + iq - ik
    in_band = (diff >= 0) & (diff <= ATTN_BLOCK)
    exists = (ik >= ATTN_BLOCK) | (blk > 0)
    valid = in_band & exists
    for hd in range(N_KV_HEADS):
        cs = slice(hd * HEAD_DIM, (hd + 1) * HEAD_DIM)
        q = q_ref[:, cs].astype(BF16)
        k = jnp.concatenate([kp_ref[:, cs], kc_ref[:, cs]], axis=0).astype(BF16)
        v = jnp.concatenate([vp_ref[:, cs], vc_ref[:, cs]], axis=0).astype(BF16)
        s = lax.dot_general(q, k, NT_DIMS, preferred_element_type=F32) * ATTN_SCALE
        s = jnp.where(valid, s, -jnp.inf)
        (p,), z, lse = _softmax_parts([s])
        o_ref[:, cs] = jnp.dot(p.astype(BF16), v, preferred_element_type=F32) / z
        lse_ref[:, cs] = jnp.broadcast_to(lse, (ATTN_BLOCK, HEAD_DIM))


def _attn_prompt_group(q, kv, group, dil):
    sub_len = SEQ // dil
    nblk = sub_len // ATTN_BLOCK
    qv = q.reshape(N_TOK // dil, dil * Q_WIDTH)
    kvv = kv.reshape(N_TOK // dil, dil * 2 * KV_WIDTH)
    n_q = N_DGROUPS

    def cur(col):
        return lambda b, r, l: (b * nblk + l, col(r))

    def prev(col):
        return lambda b, r, l: (b * nblk + jnp.maximum(l - 1, 0), col(r))

    blk = (ATTN_BLOCK, KV_WIDTH)
    out_shape = jax.ShapeDtypeStruct((N_PROMPT // dil, dil * KV_WIDTH), F32)
    o, lse = pl.pallas_call(
        _attn_prompt_kernel,
        grid=(BATCH, dil, nblk),
        in_specs=[pl.BlockSpec(blk, cur(lambda r: r * n_q + group)),
                  pl.BlockSpec(blk, prev(lambda r: 2 * r)),
                  pl.BlockSpec(blk, cur(lambda r: 2 * r)),
                  pl.BlockSpec(blk, prev(lambda r: 2 * r + 1)),
                  pl.BlockSpec(blk, cur(lambda r: 2 * r + 1))],
        out_specs=[pl.BlockSpec(blk, cur(lambda r: r))] * 2,
        out_shape=[out_shape] * 2,
        compiler_params=_params("parallel", "parallel", "parallel"),
    )(qv, kvv, kvv, kvv, kvv)
    return o.reshape(N_PROMPT, KV_WIDTH), lse.reshape(N_PROMPT, KV_WIDTH)


CACHE_PAGE = 16
CACHE_PAGES = PAST_LEN // CACHE_PAGE
G1_PAGES = DILATED_GROUPS[1][0] // CACHE_PAGE
G0_PAGES = DILATED_GROUPS[0][0] // CACHE_PAGE


def _attn_sample_kernel(q_ref, kvn_ref, ca_ref, cb_ref, o_ref):
    def iota2(n, axis):
        return lax.broadcasted_iota(jnp.int32, (DEC_SEQ, n), axis)

    n2, n1, n0 = CACHE_PAGES * DEC_SEQ, G1_PAGES * CACHE_PAGE, G0_PAGES * CACHE_PAGE
    jn, cn = iota2(DEC_SEQ, 0), iota2(DEC_SEQ, 1)
    past_valid = [
        iota2(n0, 1) >= iota2(n0, 0),
        (iota2(n1, 1) >= iota2(n1, 0)) & ((n1 + iota2(n1, 0) - iota2(n1, 1)) % 4 == 0),
        iota2(n2, 1) % DEC_SEQ == iota2(n2, 0),
    ]
    new_valid = [cn <= jn, (cn <= jn) & ((jn - cn) % 4 == 0), cn == jn]
    for hd in range(N_KV_HEADS):
        ks = slice(hd * HEAD_DIM, (hd + 1) * HEAD_DIM)
        vs = slice(KV_WIDTH + hd * HEAD_DIM, KV_WIDTH + (hd + 1) * HEAD_DIM)
        k_new = kvn_ref[0, :, ks].astype(BF16)
        v_new = kvn_ref[0, :, vs].astype(BF16)
        past = [
            (cb_ref[0, G1_PAGES - G0_PAGES:, :, ks].reshape(n0, HEAD_DIM),
             cb_ref[0, G1_PAGES - G0_PAGES:, :, vs].reshape(n0, HEAD_DIM)),
            (cb_ref[0, :, :, ks].reshape(n1, HEAD_DIM), cb_ref[0, :, :, vs].reshape(n1, HEAD_DIM)),
            (ca_ref[0, :, :, ks].reshape(n2, HEAD_DIM), ca_ref[0, :, :, vs].reshape(n2, HEAD_DIM)),
        ]
        outs, lses = [], []
        for g in range(N_DGROUPS):
            q = q_ref[0, :, g * KV_WIDTH + hd * HEAD_DIM:g * KV_WIDTH + (hd + 1) * HEAD_DIM].astype(BF16)
            k_past, v_past = past[g]
            sp = lax.dot_general(q, k_past.astype(BF16), NT_DIMS, preferred_element_type=F32) * ATTN_SCALE
            sn = lax.dot_general(q, k_new, NT_DIMS, preferred_element_type=F32) * ATTN_SCALE
            sp = jnp.where(past_valid[g], sp, -jnp.inf)
            sn = jnp.where(new_valid[g], sn, -jnp.inf)
            (pp, pn), z, lse = _softmax_parts([sp, sn])
            o = (jnp.dot(pp.astype(BF16), v_past.astype(BF16), preferred_element_type=F32)
                 + jnp.dot(pn.astype(BF16), v_new, preferred_element_type=F32)) / z
            outs.append(o)
            lses.append(lse)
        top = functools.reduce(jnp.maximum, lses)
        wts = [jnp.exp(l - top) for l in lses]
        merged = functools.reduce(jnp.add, [w * o for w, o in zip(wts, outs)])
        o_ref[0, :, ks] = merged / functools.reduce(jnp.add, wts)


def _attn_sample(q, kv, cache_kv):
    qv = q.reshape(N_TOK // DEC_SEQ, DEC_SEQ, Q_WIDTH)
    kvv = kv.reshape(N_TOK // DEC_SEQ, DEC_SEQ, 2 * KV_WIDTH)
    cache = cache_kv.reshape(DEC_BATCH, CACHE_PAGES, CACHE_PAGE, 2 * KV_WIDTH)
    first = N_PROMPT // DEC_SEQ
    o = pl.pallas_call(
        _attn_sample_kernel,
        grid=(DEC_BATCH,),
        in_specs=[pl.BlockSpec((1, DEC_SEQ, Q_WIDTH), lambda b: (first + b, 0, 0)),
                  pl.BlockSpec((1, DEC_SEQ, 2 * KV_WIDTH), lambda b: (first + b, 0, 0)),
                  pl.BlockSpec((1, CACHE_PAGES, DEC_SEQ, 2 * KV_WIDTH), lambda b: (b, 0, 0, 0)),
                  pl.BlockSpec((1, G1_PAGES, CACHE_PAGE, 2 * KV_WIDTH),
                               lambda b: (b, CACHE_PAGES // G1_PAGES - 1, 0, 0))],
        out_specs=pl.BlockSpec((1, DEC_SEQ, KV_WIDTH), lambda b: (b, 0, 0)),
        out_shape=jax.ShapeDtypeStruct((DEC_BATCH, DEC_SEQ, KV_WIDTH), F32),
        compiler_params=_params("parallel"),
    )(qv, kvv, cache, cache)
    return o.reshape(N_SAMPLE, KV_WIDTH)


def _attn_out_kernel(*refs, n_groups):
    o_refs = refs[:n_groups]
    lse_refs = refs[n_groups:2 * n_groups] if n_groups > 1 else ()
    h_ref, wo_ref, out_ref = refs[-3:]
    if n_groups > 1:
        lses = [r[...] for r in lse_refs]
        top = functools.reduce(jnp.maximum, lses)
        wts = [jnp.exp(l - top) for l in lses]
        o = functools.reduce(jnp.add, [w * r[...] for w, r in zip(wts, o_refs)])
        o = o / functools.reduce(jnp.add, wts)
    else:
        o = o_refs[0][...]
    out_ref[...] = h_ref[...] + jnp.dot(o.astype(BF16), wo_ref[...], preferred_element_type=F32)


def _attn_out(outs, lses, h, w_o, row0, n_rows):
    n_groups = len(outs)
    t0 = row0 // ROW_TILE
    return pl.pallas_call(
        functools.partial(_attn_out_kernel, n_groups=n_groups),
        grid=(n_rows // ROW_TILE,),
        in_specs=[_rows(KV_WIDTH)] * (n_groups + len(lses))
        + [pl.BlockSpec((ROW_TILE, D_MODEL), lambda i: (t0 + i, 0)), _resident((KV_WIDTH, D_MODEL))],
        out_specs=_rows(D_MODEL),
        out_shape=jax.ShapeDtypeStruct((n_rows, D_MODEL), F32),
        compiler_params=_params("parallel"),
    )(*outs, *lses, h, w_o)


def _rope_tables():
    half = HEAD_DIM // 2
    inv_freq = ROPE_THETA ** (-jnp.arange(half, dtype=F32) / half)
    pos = jnp.concatenate([jnp.tile(jnp.arange(SEQ, dtype=jnp.int32), BATCH),
                           jnp.tile(PAST_LEN + jnp.arange(DEC_SEQ, dtype=jnp.int32), DEC_BATCH)])
    ang = pos.astype(F32)[:, None] * inv_freq[None, :]
    cos, sin = jnp.cos(ang), jnp.sin(ang)
    return jnp.concatenate([cos, cos], axis=-1), jnp.concatenate([-sin, sin], axis=-1)


def _mix_weights(w_s, b_s):
    tril = jnp.tril(jnp.ones((CHUNK, CHUNK), dtype=bool))
    w_prompt = jnp.where(tril, w_s, 0.0)
    reps = CHUNK // DEC_SEQ
    small = w_prompt[:, :DEC_SEQ, :DEC_SEQ]
    eye = jnp.eye(reps, dtype=F32)
    w_sample = jnp.einsum('ab,gts->gatbs', eye, small).reshape(A_GROUPS, CHUNK, CHUNK)
    wmix = jnp.stack([w_prompt, w_sample]).astype(BF16)
    b_prompt = b_s.T
    b_sample = jnp.tile(b_s[:, :DEC_SEQ].T, (reps, 1))
    bias = jnp.repeat(jnp.stack([b_prompt, b_sample]), A_GROUP_DIM, axis=-1)
    return wmix, bias


def kernel(x_prompt, x_sample, p_prompt, p_sample, cache_kv, g_mix, a_w_in, a_ln_g, a_ln_b, a_w_s, a_b_s,
           a_w_out, g_kv, w_kv, b_w_q, b_w_o, g_ffn, peer_w_q, peer_keys, peer_u, peer_v, g_ple,
           ple_w_gate, ple_w_proj, g_final):
    h = jnp.concatenate([x_prompt.reshape(N_PROMPT, D_MODEL), x_sample.reshape(N_SAMPLE, D_MODEL)])
    p = jnp.concatenate([p_prompt.reshape(2, N_PROMPT, PLE_DIM), p_sample.reshape(2, N_SAMPLE, PLE_DIM)], axis=1)
    cos, sin = _rope_tables()
    row = lambda a: a.reshape(1, -1)

    u, v = _gmlp_in(h, row(g_mix[0]), a_w_in[0].astype(BF16), row(a_ln_g[0]), row(a_ln_b[0]))
    wmix, bias = _mix_weights(a_w_s[0], a_b_s[0])
    h = _gmlp_out(u, v, h, wmix, bias, a_w_out[0].astype(BF16))
    y = _peer(h, row(g_ffn[0]), peer_w_q[0], peer_keys[0], peer_u[0], peer_v[0])
    (h,) = _ple(h, y, p[0], row(g_ple[0]), ple_w_gate[0].astype(BF16), ple_w_proj[0].astype(BF16))

    kv = _proj_rope(h, row(g_kv), w_kv.astype(BF16), cos, sin, N_KV_HEADS)
    q = _proj_rope(h, row(g_mix[1]), b_w_q[0].astype(BF16), cos, sin, N_DGROUPS * N_KV_HEADS)

    outs, lses = [], []
    for g, (_, dil) in enumerate(DILATED_GROUPS):
        o, lse = _attn_prompt_group(q, kv, g, dil)
        outs.append(o)
        lses.append(lse)
    w_o = b_w_o[0].astype(BF16)
    h_prompt = _attn_out(outs, lses, h, w_o, 0, N_PROMPT)
    h_sample = _attn_out([_attn_sample(q, kv, cache_kv)], [], h, w_o, N_PROMPT, N_SAMPLE)
    h = jnp.concatenate([h_prompt, h_sample])
    y = _peer(h, row(g_ffn[1]), peer_w_q[1], peer_keys[1], peer_u[1], peer_v[1])
    h, y_out = _ple(h, y, p[1], row(g_ple[1]), ple_w_gate[1].astype(BF16), ple_w_proj[1].astype(BF16),
                    row(g_final))

    y_prompt = y_out[:N_PROMPT].reshape(BATCH, SEQ, D_MODEL)
    y_sample = y_out[N_PROMPT:].reshape(DEC_BATCH, DEC_SEQ, D_MODEL)
    a_v_prompt = v[:N_PROMPT].reshape(BATCH, SEQ, A_WIDTH)[None, :, SEQ - CHUNK:]
    a_v_sample = v[N_PROMPT:].reshape(1, DEC_BATCH, DEC_SEQ, A_WIDTH)
    kv_prompt = kv[:N_PROMPT].reshape(BATCH, SEQ, 2, N_KV_HEADS, HEAD_DIM)
    kv_sample = kv[N_PROMPT:].reshape(DEC_BATCH, DEC_SEQ, 2, N_KV_HEADS, HEAD_DIM)
    return (y_prompt, y_sample, a_v_prompt, a_v_sample, kv_prompt, kv_sample)
```

```python
import functools

import jax
import jax.numpy as jnp
from jax import lax
from jax.experimental import pallas as pl
from jax.experimental.pallas import tpu as pltpu

F32 = jnp.float32
BF16 = jnp.bfloat16

D_MODEL = 2048
BATCH = 4
SEQ = 2048
DEC_BATCH = 128
DEC_SEQ = 8
PAST_LEN = 2048
N_PROMPT = BATCH * SEQ
N_SAMPLE = DEC_BATCH * DEC_SEQ
N_TOK = N_PROMPT + N_SAMPLE

CHUNK = 128
A_WIDTH = D_MODEL
A_GROUPS = 8
A_GROUP_DIM = A_WIDTH // A_GROUPS

HEAD_DIM = 128
N_KV_HEADS = 8
KV_WIDTH = N_KV_HEADS * HEAD_DIM
DILATED_GROUPS = ((128, 1), (512, 4), (2048, 16))
N_DGROUPS = len(DILATED_GROUPS)
Q_WIDTH = N_DGROUPS * KV_WIDTH
ATTN_BLOCK = 128
ROPE_THETA = 10000.0
ATTN_SCALE = HEAD_DIM ** -0.5

N_KEYS = 128
N_EXPERTS = N_KEYS * N_KEYS
PEER_HEADS = 8
PEER_HALF = 128
PEER_TOPK = 16
PLE_DIM = 256
RMS_EPS = 1e-6
LN_EPS = 1e-5

V7X_VMEM_BYTES = 64 * 1024 * 1024
VMEM_LIMIT = V7X_VMEM_BYTES - 8 * 1024 * 1024
LANES = 128

ROW_TILE = 256
PEER_TOK_TILE = 512
PEER_EXP_TILE = 1024
PEER_I1_PER_STEP = PEER_EXP_TILE // N_KEYS

NT_DIMS = (((1,), (1,)), ((), ()))


def _params(*sem):
    return pltpu.CompilerParams(dimension_semantics=sem, vmem_limit_bytes=VMEM_LIMIT)


def _resident(shape):
    return pl.BlockSpec(shape, lambda *_: (0,) * len(shape), pipeline_mode=pl.Buffered(1))


def _rows(width, tile=ROW_TILE):
    return pl.BlockSpec((tile, width), lambda i: (i, 0))


def _rms(x, g):
    return x * lax.rsqrt(jnp.mean(x * x, axis=-1, keepdims=True) + RMS_EPS) * g


def _gmlp_in_kernel(h_ref, g_ref, w_ref, lng_ref, lnb_ref, u_ref, v_ref):
    xn = _rms(h_ref[...], g_ref[...]).astype(BF16)
    u_ref[...] = jax.nn.gelu(jnp.dot(xn, w_ref[:, :A_WIDTH], preferred_element_type=F32))
    v = jax.nn.gelu(jnp.dot(xn, w_ref[:, A_WIDTH:], preferred_element_type=F32))
    mu = jnp.mean(v, axis=-1, keepdims=True)
    vc = v - mu
    var = jnp.mean(vc * vc, axis=-1, keepdims=True)
    v_ref[...] = vc * lax.rsqrt(var + LN_EPS) * lng_ref[...] + lnb_ref[...]


def _gmlp_in(h, g, w_in, ln_g, ln_b):
    return pl.pallas_call(
        _gmlp_in_kernel,
        grid=(N_TOK // ROW_TILE,),
        in_specs=[_rows(D_MODEL), _resident((1, D_MODEL)), _resident((D_MODEL, 2 * A_WIDTH)),
                  _resident((1, A_WIDTH)), _resident((1, A_WIDTH))],
        out_specs=[_rows(A_WIDTH), _rows(A_WIDTH)],
        out_shape=[jax.ShapeDtypeStruct((N_TOK, A_WIDTH), F32)] * 2,
        compiler_params=_params("parallel"),
    )(h, g, w_in, ln_g, ln_b)


def _gmlp_out_kernel(u_ref, v_ref, h_ref, wmix_ref, bias_ref, wout_ref, o_ref, um_ref):
    for c in range(ROW_TILE // CHUNK):
        rows = slice(c * CHUNK, (c + 1) * CHUNK)
        for g in range(A_GROUPS):
            cols = slice(g * A_GROUP_DIM, (g + 1) * A_GROUP_DIM)
            mixed = jnp.dot(wmix_ref[0, g], v_ref[rows, cols].astype(BF16),
                            preferred_element_type=F32) + bias_ref[0, :, cols]
            um_ref[rows, cols] = (u_ref[rows, cols] * mixed).astype(BF16)
    o_ref[...] = h_ref[...] + jnp.dot(um_ref[...], wout_ref[...], preferred_element_type=F32)


def _gmlp_out(u, v, h, wmix, bias, w_out):
    prompt_tiles = N_PROMPT // ROW_TILE
    return pl.pallas_call(
        _gmlp_out_kernel,
        grid=(N_TOK // ROW_TILE,),
        in_specs=[_rows(A_WIDTH), _rows(A_WIDTH), _rows(D_MODEL),
                  pl.BlockSpec((1, A_GROUPS, CHUNK, CHUNK), lambda i: (i // prompt_tiles, 0, 0, 0)),
                  pl.BlockSpec((1, CHUNK, A_WIDTH), lambda i: (i // prompt_tiles, 0, 0)),
                  _resident((A_WIDTH, D_MODEL))],
        out_specs=_rows(D_MODEL),
        out_shape=jax.ShapeDtypeStruct((N_TOK, D_MODEL), F32),
        scratch_shapes=[pltpu.VMEM((ROW_TILE, A_WIDTH), BF16)],
        compiler_params=_params("parallel"),
    )(u, v, h, wmix, bias, w_out)


PEER_NTOP = PEER_TOPK + 1
TOP_SECOND = 24
TOP_ROWS = 48


def _top_values(s, top_ref, base):
    work = s
    for k in range(PEER_NTOP):
        m = jnp.max(work, axis=0, keepdims=True)
        top_ref[base + k:base + k + 1, :] = m
        work = jnp.where(work == m, -jnp.inf, work)


def _peer_route_kernel(h_ref, g_ref, wpq_ref, keys_ref, xn_ref, c1_ref, g1_ref, g2_ref, top_ref):
    xn = _rms(h_ref[...], g_ref[...]).astype(BF16)
    xn_ref[...] = xn
    q = jnp.dot(xn, wpq_ref[...], preferred_element_type=F32).astype(BF16)
    top_ref[...] = jnp.full(top_ref.shape, -jnp.inf, F32)
    for hd in range(PEER_HEADS):
        c0 = hd * 2 * PEER_HALF
        s1 = lax.dot_general(keys_ref[hd, 0], q[:, c0:c0 + PEER_HALF], NT_DIMS,
                             preferred_element_type=F32)
        s2 = lax.dot_general(keys_ref[hd, 1], q[:, c0 + PEER_HALF:c0 + 2 * PEER_HALF], NT_DIMS,
                             preferred_element_type=F32)
        _top_values(s1, top_ref, 0)
        _top_values(s2, top_ref, TOP_SECOND)
        m1 = top_ref[0:1, :]
        m2 = top_ref[TOP_SECOND:TOP_SECOND + 1, :]
        second_lo = top_ref[TOP_SECOND:TOP_SECOND + 8, :]
        cands = [m1 + second_lo, m1 + top_ref[TOP_SECOND + 8:TOP_SECOND + 16, :],
                 m1 + top_ref[TOP_SECOND + 16:TOP_SECOND + 24, :]]
        for a in range(1, PEER_NTOP):
            cands.append(top_ref[a:a + 1, :] + second_lo)
        top = m1 + m2
        zsum = jnp.zeros_like(top)
        ranked = []
        for k in range(PEER_NTOP):
            m = jnp.max(functools.reduce(jnp.maximum, cands), axis=0, keepdims=True)
            ranked.append(m)
            if k < PEER_TOPK:
                zsum = zsum + jnp.exp(m - top)
            if k + 1 < PEER_NTOP:
                cands = [jnp.where(c == m, -jnp.inf, c) for c in cands]
        thr = 0.5 * (ranked[PEER_TOPK - 1] + ranked[PEER_TOPK])
        c1_ref[hd] = jnp.exp((thr - m2) - s1)
        g1_ref[hd] = jnp.exp(s1 - m1) / zsum
        g2_ref[hd] = jnp.exp(s2 - m2)


def _peer_route(h, g, w_pq, keys):
    route_spec = pl.BlockSpec((PEER_HEADS, N_KEYS, ROW_TILE), lambda i: (0, 0, i))
    route_shape = jax.ShapeDtypeStruct((PEER_HEADS, N_KEYS, N_TOK), F32)
    return pl.pallas_call(
        _peer_route_kernel,
        grid=(N_TOK // ROW_TILE,),
        in_specs=[_rows(D_MODEL), _resident((1, D_MODEL)), _resident((D_MODEL, D_MODEL)),
                  _resident((PEER_HEADS, 2, N_KEYS, PEER_HALF))],
        out_specs=[_rows(D_MODEL)] + [route_spec] * 3,
        out_shape=[jax.ShapeDtypeStruct((N_TOK, D_MODEL), BF16)] + [route_shape] * 3,
        scratch_shapes=[pltpu.VMEM((TOP_ROWS, ROW_TILE), F32)],
        compiler_params=_params("parallel"),
    )(h, g, w_pq, keys)


PEER_TOK_STEPS = N_TOK // PEER_TOK_TILE
PEER_EXP_STEPS = N_EXPERTS // PEER_EXP_TILE
PEER_STEPS = PEER_TOK_STEPS * PEER_EXP_STEPS
PEER_LAG = 2
PEER_SUB_KEYS = 32


def _peer_dense_kernel(x_ref, u_ref, vt_ref, c1_ref, g1_ref, g2_ref, y_ref,
                       acc_ref, act_a_ref, act_b_ref, w_a_ref, w_b_ref):
    s = pl.program_id(0)
    contract_exp_step = jnp.maximum(s - PEER_LAG, 0) % PEER_EXP_STEPS

    @pl.when(s == 0)
    def _():
        act_b_ref[...] = jnp.zeros_like(act_b_ref)
        w_a_ref[...] = jnp.zeros_like(w_a_ref)

    @pl.when(contract_exp_step == 0)
    def _():
        acc_ref[...] = jnp.zeros_like(acc_ref)

    def step(act_new_ref, act_old_ref, w_new_ref, w_old_ref):
        act_new_ref[...] = lax.dot_general(u_ref[...], x_ref[...], NT_DIMS, preferred_element_type=F32)
        acc_ref[...] += jnp.dot(vt_ref[...], w_old_ref[...], preferred_element_type=F32)
        for ii in range(PEER_I1_PER_STEP):
            for lt in range(PEER_TOK_TILE // LANES):
                ls = slice(lt * LANES, (lt + 1) * LANES)
                for k0 in range(0, N_KEYS, PEER_SUB_KEYS):
                    ks = slice(k0, k0 + PEER_SUB_KEYS)
                    rs = slice(ii * N_KEYS + k0, ii * N_KEYS + k0 + PEER_SUB_KEYS)
                    coef = None
                    for hd in range(PEER_HEADS):
                        g2 = g2_ref[hd, ks, ls]
                        term = jnp.where(g2 >= c1_ref[hd, ii:ii + 1, ls], g2, 0.0) * g1_ref[hd, ii:ii + 1, ls]
                        coef = term if coef is None else coef + term
                    w_new_ref[rs, ls] = (jax.nn.gelu(act_old_ref[rs, ls]) * coef).astype(BF16)

    @pl.when(s % 2 == 0)
    def _():
        step(act_a_ref, act_b_ref, w_b_ref, w_a_ref)

    @pl.when(s % 2 == 1)
    def _():
        step(act_b_ref, act_a_ref, w_a_ref, w_b_ref)

    @pl.when((contract_exp_step == PEER_EXP_STEPS - 1) & (s >= PEER_LAG))
    def _():
        y_ref[...] = acc_ref[...].T


def _peer_dense(xn, u_bf, vt_bf, c1, g1, g2):
    def tile(s, lag):
        return jnp.clip(s - lag, 0, PEER_STEPS - 1)

    def tok(s, lag):
        return tile(s, lag) // PEER_EXP_STEPS

    def exp(s, lag):
        return tile(s, lag) % PEER_EXP_STEPS

    first_spec = pl.BlockSpec((PEER_HEADS, PEER_I1_PER_STEP, PEER_TOK_TILE), lambda s: (0, exp(s, 1), tok(s, 1)))
    act_shape = pltpu.VMEM((PEER_EXP_TILE, PEER_TOK_TILE), F32)
    w_shape = pltpu.VMEM((PEER_EXP_TILE, PEER_TOK_TILE), BF16)
    return pl.pallas_call(
        _peer_dense_kernel,
        grid=(PEER_STEPS + PEER_LAG,),
        in_specs=[pl.BlockSpec((PEER_TOK_TILE, D_MODEL), lambda s: (tok(s, 0), 0)),
                  pl.BlockSpec((PEER_EXP_TILE, D_MODEL), lambda s: (exp(s, 0), 0)),
                  pl.BlockSpec((D_MODEL, PEER_EXP_TILE), lambda s: (0, exp(s, PEER_LAG))),
                  first_spec, first_spec,
                  pl.BlockSpec((PEER_HEADS, N_KEYS, PEER_TOK_TILE), lambda s: (0, 0, tok(s, 1)))],
        out_specs=pl.BlockSpec((PEER_TOK_TILE, D_MODEL), lambda s: (tok(s, PEER_LAG), 0)),
        out_shape=jax.ShapeDtypeStruct((N_TOK, D_MODEL), F32),
        scratch_shapes=[pltpu.VMEM((D_MODEL, PEER_TOK_TILE), F32), act_shape, act_shape, w_shape, w_shape],
        compiler_params=_params("arbitrary"),
    )(xn, u_bf, vt_bf, c1, g1, g2)


def _peer(h, g, w_pq, keys, u_tab, v_tab):
    xn, c1, g1, g2 = _peer_route(h, g, w_pq.astype(BF16), keys.astype(BF16))
    return _peer_dense(xn, u_tab.astype(BF16), v_tab.T.astype(BF16), c1, g1, g2)


def _ple_kernel(h_ref, y_ref, p_ref, g_ref, wg_ref, wp_ref, *rest, final):
    h = h_ref[...] + y_ref[...]
    gate = jax.nn.sigmoid(jnp.dot(_rms(h, g_ref[...]).astype(BF16), wg_ref[...], preferred_element_type=F32))
    out = h + gate * jnp.dot(p_ref[...].astype(BF16), wp_ref[...], preferred_element_type=F32)
    if final:
        gf_ref, o_ref, yf_ref = rest
        yf_ref[...] = _rms(out, gf_ref[...])
    else:
        (o_ref,) = rest
    o_ref[...] = out


def _ple(h, y, p, g, w_gate, w_proj, g_final=None):
    final = g_final is not None
    in_specs = [_rows(D_MODEL), _rows(D_MODEL), _rows(PLE_DIM), _resident((1, D_MODEL)),
                _resident((D_MODEL, D_MODEL)), _resident((PLE_DIM, D_MODEL))]
    args = [h, y, p, g, w_gate, w_proj]
    n_out = 1
    if final:
        in_specs.append(_resident((1, D_MODEL)))
        args.append(g_final)
        n_out = 2
    return pl.pallas_call(
        functools.partial(_ple_kernel, final=final),
        grid=(N_TOK // ROW_TILE,),
        in_specs=in_specs,
        out_specs=[_rows(D_MODEL)] * n_out,
        out_shape=[jax.ShapeDtypeStruct((N_TOK, D_MODEL), F32)] * n_out,
        compiler_params=_params("parallel"),
    )(*args)


def _proj_rope_kernel(h_ref, g_ref, w_ref, cos_ref, sin_ref, o_ref, *, n_rope_heads, width):
    xn = _rms(h_ref[...], g_ref[...]).astype(BF16)
    cos = cos_ref[...]
    sin = sin_ref[...]
    for c0 in range(0, width, KV_WIDTH):
        z = jnp.dot(xn, w_ref[:, c0:c0 + KV_WIDTH], preferred_element_type=F32)
        for hd in range(N_KV_HEADS):
            cs = slice(hd * HEAD_DIM, (hd + 1) * HEAD_DIM)
            zh = z[:, cs]
            if c0 // HEAD_DIM + hd < n_rope_heads:
                zh = zh * cos + pltpu.roll(zh, HEAD_DIM // 2, 1) * sin
            o_ref[:, c0 + hd * HEAD_DIM:c0 + (hd + 1) * HEAD_DIM] = zh


def _proj_rope(h, g, w, cos, sin, n_rope_heads):
    width = w.shape[1]
    return pl.pallas_call(
        functools.partial(_proj_rope_kernel, n_rope_heads=n_rope_heads, width=width),
        grid=(N_TOK // ROW_TILE,),
        in_specs=[_rows(D_MODEL), _resident((1, D_MODEL)), _resident((D_MODEL, width)),
                  _rows(HEAD_DIM), _rows(HEAD_DIM)],
        out_specs=_rows(width),
        out_shape=jax.ShapeDtypeStruct((N_TOK, width), F32),
        compiler_params=_params("parallel"),
    )(h, g, w, cos, sin)


def _softmax_parts(scores):
    m = functools.reduce(jnp.maximum, [jnp.max(s, axis=-1, keepdims=True) for s in scores])
    probs = [jnp.exp(s - m) for s in scores]
    z = functools.reduce(jnp.add, [jnp.sum(p, axis=-1, keepdims=True) for p in probs])
    return probs, z, m + jnp.log(z)


def _attn_prompt_kernel(q_ref, kp_ref, kc_ref, vp_ref, vc_ref, o_ref, lse_ref):
    blk = pl.program_id(2)
    iq = lax.broadcasted_iota(jnp.int32, (ATTN_BLOCK, 2 * ATTN_BLOCK), 0)
    ik = lax.broadcasted_iota(jnp.int32, (ATTN_BLOCK, 2 * ATTN_BLOCK), 1)
    diff = ATTN_BLOCK + iq - ik
    in_band = (diff >= 0) & (diff <= ATTN_BLOCK)
    exists = (ik >= ATTN_BLOCK) | (blk > 0)
    valid = in_band & exists
    for hd in range(N_KV_HEADS):
        cs = slice(hd * HEAD_DIM, (hd + 1) * HEAD_DIM)
        q = q_ref[:, cs].astype(BF16)
        k = jnp.concatenate([kp_ref[:, cs], kc_ref[:, cs]], axis=0).astype(BF16)
        v = jnp.concatenate([vp_ref[:, cs], vc_ref[:, cs]], axis=0).astype(BF16)
        s = lax.dot_general(q, k, NT_DIMS, preferred_element_type=F32) * ATTN_SCALE
        s = jnp.where(valid, s, -jnp.inf)
        (p,), z, lse = _softmax_parts([s])
        o_ref[:, cs] = jnp.dot(p.astype(BF16), v, preferred_element_type=F32) / z
        lse_ref[:, cs] = jnp.broadcast_to(lse, (ATTN_BLOCK, HEAD_DIM))


def _attn_prompt_group(q, kv, group, dil):
    sub_len = SEQ // dil
    nblk = sub_len // ATTN_BLOCK
    qv = q.reshape(N_TOK // dil, dil * Q_WIDTH)
    kvv = kv.reshape(N_TOK // dil, dil * 2 * KV_WIDTH)
    n_q = N_DGROUPS

    def cur(col):
        return lambda b, r, l: (b * nblk + l, col(r))

    def prev(col):
        return lambda b, r, l: (b * nblk + jnp.maximum(l - 1, 0), col(r))

    blk = (ATTN_BLOCK, KV_WIDTH)
    out_shape = jax.ShapeDtypeStruct((N_PROMPT // dil, dil * KV_WIDTH), F32)
    o, lse = pl.pallas_call(
        _attn_prompt_kernel,
        grid=(BATCH, dil, nblk),
        in_specs=[pl.BlockSpec(blk, cur(lambda r: r * n_q + group)),
                  pl.BlockSpec(blk, prev(lambda r: 2 * r)),
                  pl.BlockSpec(blk, cur(lambda r: 2 * r)),
                  pl.BlockSpec(blk, prev(lambda r: 2 * r + 1)),
                  pl.BlockSpec(blk, cur(lambda r: 2 * r + 1))],
        out_specs=[pl.BlockSpec(blk, cur(lambda r: r))] * 2,
        out_shape=[out_shape] * 2,
        compiler_params=_params("parallel", "parallel", "parallel"),
    )(qv, kvv, kvv, kvv, kvv)
    return o.reshape(N_PROMPT, KV_WIDTH), lse.reshape(N_PROMPT, KV_WIDTH)


CACHE_PAGE = 16
CACHE_PAGES = PAST_LEN // CACHE_PAGE
G1_PAGES = DILATED_GROUPS[1][0] // CACHE_PAGE
G0_PAGES = DILATED_GROUPS[0][0] // CACHE_PAGE


G2_OLD_PAGES = CACHE_PAGES - G1_PAGES
QH = DEC_SEQ * N_KV_HEADS


def _sample_masks():
    row = jnp.arange(QH)[:, None]
    j, h = row // N_KV_HEADS, row % N_KV_HEADS

    def mask(n_pos, ok):
        col = jnp.arange(n_pos * N_KV_HEADS)[None, :]
        pos, hk = col // N_KV_HEADS, col % N_KV_HEADS
        return jnp.where((hk == h) & ok(pos), 0.0, -jnp.inf).astype(F32)

    n0, n1 = G0_PAGES * CACHE_PAGE, G1_PAGES * CACHE_PAGE
    past = [mask(n0, lambda c: c >= j),
            mask(n1, lambda c: (c >= j) & ((n1 + j - c) % DILATED_GROUPS[1][1] == 0)),
            mask(CACHE_PAGES * DEC_SEQ, lambda p: p % DEC_SEQ == j)]
    new = [mask(DEC_SEQ, lambda c: c <= j),
           mask(DEC_SEQ, lambda c: (c <= j) & ((j - c) % DILATED_GROUPS[1][1] == 0)),
           mask(DEC_SEQ, lambda c: c == j)]
    return past, jnp.stack(new)


def _attn_sample_kernel(q_ref, kvn_ref, ca_ref, cb_ref, m0_ref, m1_ref, m2_ref, mnew_ref, o_ref):
    def keys_values(ref, *idx):
        return tuple(ref[idx + (kv,)].reshape(-1, HEAD_DIM).astype(BF16) for kv in range(2))

    n_old = G2_OLD_PAGES * DEC_SEQ * N_KV_HEADS
    new = keys_values(kvn_ref, 0, slice(None))
    parts = [
        [(keys_values(cb_ref, 0, slice(G1_PAGES - G0_PAGES, None), slice(None)), m0_ref[...])],
        [(keys_values(cb_ref, 0, slice(None), slice(None)), m1_ref[...])],
        [(keys_values(ca_ref, 0, slice(None), slice(None)), m2_ref[:, :n_old]),
         (keys_values(cb_ref, 0, slice(None), slice(0, DEC_SEQ)), m2_ref[:, n_old:])],
    ]
    outs, lses = [], []
    for g in range(N_DGROUPS):
        q = q_ref[0, g].reshape(QH, HEAD_DIM).astype(BF16)
        group = parts[g] + [(new, mnew_ref[g])]
        scores = [lax.dot_general(q, k, NT_DIMS, preferred_element_type=F32) * ATTN_SCALE + m
                  for (k, _), m in group]
        probs, z, lse = _softmax_parts(scores)
        o = functools.reduce(jnp.add, [jnp.dot(p.astype(BF16), v, preferred_element_type=F32)
                                       for p, ((_, v), _) in zip(probs, group)])
        outs.append(o / z)
        lses.append(lse)
    top = functools.reduce(jnp.maximum, lses)
    wts = [jnp.exp(l - top) for l in lses]
    merged = functools.reduce(jnp.add, [w * o for w, o in zip(wts, outs)])
    o_ref[0] = (merged / functools.reduce(jnp.add, wts)).reshape(DEC_SEQ, N_KV_HEADS, HEAD_DIM)


def _attn_sample(q_sample, kv_sample, cache_kv):
    q5 = q_sample.reshape(DEC_BATCH, DEC_SEQ, N_DGROUPS, N_KV_HEADS, HEAD_DIM).transpose(0, 2, 1, 3, 4)
    cache = cache_kv.reshape(DEC_BATCH, CACHE_PAGES, CACHE_PAGE, 2, N_KV_HEADS, HEAD_DIM)
    (m0, m1, m2), m_new = _sample_masks()
    tile_dims = (2, N_KV_HEADS, HEAD_DIM)
    o = pl.pallas_call(
        _attn_sample_kernel,
        grid=(DEC_BATCH,),
        in_specs=[pl.BlockSpec((1, N_DGROUPS, DEC_SEQ, N_KV_HEADS, HEAD_DIM), lambda b: (b, 0, 0, 0, 0)),
                  pl.BlockSpec((1, DEC_SEQ) + tile_dims, lambda b: (b, 0, 0, 0, 0)),
                  pl.BlockSpec((1, G2_OLD_PAGES, DEC_SEQ) + tile_dims, lambda b: (b, 0, 0, 0, 0, 0)),
                  pl.BlockSpec((1, G1_PAGES, CACHE_PAGE) + tile_dims,
                               lambda b: (b, CACHE_PAGES // G1_PAGES - 1, 0, 0, 0, 0)),
                  _resident(m0.shape), _resident(m1.shape), _resident(m2.shape), _resident(m_new.shape)],
        out_specs=pl.BlockSpec((1, DEC_SEQ, N_KV_HEADS, HEAD_DIM), lambda b: (b, 0, 0, 0)),
        out_shape=jax.ShapeDtypeStruct((DEC_BATCH, DEC_SEQ, N_KV_HEADS, HEAD_DIM), F32),
        compiler_params=_params("parallel"),
    )(q5, kv_sample, cache, cache, m0, m1, m2, m_new)
    return o.reshape(N_SAMPLE, KV_WIDTH)


def _attn_out_kernel(*refs, n_groups):
    o_refs = refs[:n_groups]
    lse_refs = refs[n_groups:2 * n_groups] if n_groups > 1 else ()
    h_ref, wo_ref, out_ref = refs[-3:]
    if n_groups > 1:
        lses = [r[...] for r in lse_refs]
        top = functools.reduce(jnp.maximum, lses)
        wts = [jnp.exp(l - top) for l in lses]
        o = functools.reduce(jnp.add, [w * r[...] for w, r in zip(wts, o_refs)])
        o = o / functools.reduce(jnp.add, wts)
    else:
        o = o_refs[0][...]
    out_ref[...] = h_ref[...] + jnp.dot(o.astype(BF16), wo_ref[...], preferred_element_type=F32)


def _attn_out(outs, lses, h, w_o, row0, n_rows):
    n_groups = len(outs)
    t0 = row0 // ROW_TILE
    return pl.pallas_call(
        functools.partial(_attn_out_kernel, n_groups=n_groups),
        grid=(n_rows // ROW_TILE,),
        in_specs=[_rows(KV_WIDTH)] * (n_groups + len(lses))
        + [pl.BlockSpec((ROW_TILE, D_MODEL), lambda i: (t0 + i, 0)), _resident((KV_WIDTH, D_MODEL))],
        out_specs=_rows(D_MODEL),
        out_shape=jax.ShapeDtypeStruct((n_rows, D_MODEL), F32),
        compiler_params=_params("parallel"),
    )(*outs, *lses, h, w_o)


def _rope_tables():
    half = HEAD_DIM // 2
    inv_freq = ROPE_THETA ** (-jnp.arange(half, dtype=F32) / half)
    pos = jnp.concatenate([jnp.tile(jnp.arange(SEQ, dtype=jnp.int32), BATCH),
                           jnp.tile(PAST_LEN + jnp.arange(DEC_SEQ, dtype=jnp.int32), DEC_BATCH)])
    ang = pos.astype(F32)[:, None] * inv_freq[None, :]
    cos, sin = jnp.cos(ang), jnp.sin(ang)
    return jnp.concatenate([cos, cos], axis=-1), jnp.concatenate([-sin, sin], axis=-1)


def _mix_weights(w_s, b_s):
    tril = jnp.tril(jnp.ones((CHUNK, CHUNK), dtype=bool))
    w_prompt = jnp.where(tril, w_s, 0.0)
    reps = CHUNK // DEC_SEQ
    small = w_prompt[:, :DEC_SEQ, :DEC_SEQ]
    eye = jnp.eye(reps, dtype=F32)
    w_sample = jnp.einsum('ab,gts->gatbs', eye, small).reshape(A_GROUPS, CHUNK, CHUNK)
    wmix = jnp.stack([w_prompt, w_sample]).astype(BF16)
    b_prompt = b_s.T
    b_sample = jnp.tile(b_s[:, :DEC_SEQ].T, (reps, 1))
    bias = jnp.repeat(jnp.stack([b_prompt, b_sample]), A_GROUP_DIM, axis=-1)
    return wmix, bias


def kernel(x_prompt, x_sample, p_prompt, p_sample, cache_kv, g_mix, a_w_in, a_ln_g, a_ln_b, a_w_s, a_b_s,
           a_w_out, g_kv, w_kv, b_w_q, b_w_o, g_ffn, peer_w_q, peer_keys, peer_u, peer_v, g_ple,
           ple_w_gate, ple_w_proj, g_final):
    h = jnp.concatenate([x_prompt.reshape(N_PROMPT, D_MODEL), x_sample.reshape(N_SAMPLE, D_MODEL)])
    p = jnp.concatenate([p_prompt.reshape(2, N_PROMPT, PLE_DIM), p_sample.reshape(2, N_SAMPLE, PLE_DIM)], axis=1)
    cos, sin = _rope_tables()
    row = lambda a: a.reshape(1, -1)

    u, v = _gmlp_in(h, row(g_mix[0]), a_w_in[0].astype(BF16), row(a_ln_g[0]), row(a_ln_b[0]))
    wmix, bias = _mix_weights(a_w_s[0], a_b_s[0])
    h = _gmlp_out(u, v, h, wmix, bias, a_w_out[0].astype(BF16))
    y = _peer(h, row(g_ffn[0]), peer_w_q[0], peer_keys[0], peer_u[0], peer_v[0])
    (h,) = _ple(h, y, p[0], row(g_ple[0]), ple_w_gate[0].astype(BF16), ple_w_proj[0].astype(BF16))

    kv = _proj_rope(h, row(g_kv), w_kv.astype(BF16), cos, sin, N_KV_HEADS)
    q = _proj_rope(h, row(g_mix[1]), b_w_q[0].astype(BF16), cos, sin, N_DGROUPS * N_KV_HEADS)

    outs, lses = [], []
    for g, (_, dil) in enumerate(DILATED_GROUPS):
        o, lse = _attn_prompt_group(q, kv, g, dil)
        outs.append(o)
        lses.append(lse)
    w_o = b_w_o[0].astype(BF16)
    h_prompt = _attn_out(outs, lses, h, w_o, 0, N_PROMPT)
    kv_sample = kv[N_PROMPT:].reshape(DEC_BATCH, DEC_SEQ, 2, N_KV_HEADS, HEAD_DIM)
    o_sample = _attn_sample(q[N_PROMPT:].reshape(DEC_BATCH, DEC_SEQ, Q_WIDTH), kv_sample, cache_kv)
    h_sample = _attn_out([o_sample], [], h, w_o, N_PROMPT, N_SAMPLE)
    h = jnp.concatenate([h_prompt, h_sample])
    y = _peer(h, row(g_ffn[1]), peer_w_q[1], peer_keys[1], peer_u[1], peer_v[1])
    h, y_out = _ple(h, y, p[1], row(g_ple[1]), ple_w_gate[1].astype(BF16), ple_w_proj[1].astype(BF16),
                    row(g_final))

    y_prompt = y_out[:N_PROMPT].reshape(BATCH, SEQ, D_MODEL)
    y_sample = y_out[N_PROMPT:].reshape(DEC_BATCH, DEC_SEQ, D_MODEL)
    a_v_prompt = v[:N_PROMPT].reshape(BATCH, SEQ, A_WIDTH)[None, :, SEQ - CHUNK:]
    a_v_sample = v[N_PROMPT:].reshape(1, DEC_BATCH, DEC_SEQ, A_WIDTH)
    kv_prompt = kv[:N_PROMPT].reshape(BATCH, SEQ, 2, N_KV_HEADS, HEAD_DIM)
    return (y_prompt, y_sample, a_v_prompt, a_v_sample, kv_prompt, kv_sample)
```

```python
import functools

import jax
import jax.numpy as jnp
from jax import lax
from jax.experimental import pallas as pl
from jax.experimental.pallas import tpu as pltpu

F32 = jnp.float32
BF16 = jnp.bfloat16

D_MODEL = 2048
BATCH = 4
SEQ = 2048
DEC_BATCH = 128
DEC_SEQ = 8
PAST_LEN = 2048
N_PROMPT = BATCH * SEQ
N_SAMPLE = DEC_BATCH * DEC_SEQ
N_TOK = N_PROMPT + N_SAMPLE

CHUNK = 128
A_WIDTH = D_MODEL
A_GROUPS = 8
A_GROUP_DIM = A_WIDTH // A_GROUPS

HEAD_DIM = 128
N_KV_HEADS = 8
KV_WIDTH = N_KV_HEADS * HEAD_DIM
DILATED_GROUPS = ((128, 1), (512, 4), (2048, 16))
N_DGROUPS = len(DILATED_GROUPS)
Q_WIDTH = N_DGROUPS * KV_WIDTH
ATTN_BLOCK = 128
ROPE_THETA = 10000.0
ATTN_SCALE = HEAD_DIM ** -0.5

N_KEYS = 128
N_EXPERTS = N_KEYS * N_KEYS
PEER_HEADS = 8
PEER_HALF = 128
PEER_TOPK = 16
PLE_DIM = 256
RMS_EPS = 1e-6
LN_EPS = 1e-5

V7X_VMEM_BYTES = 64 * 1024 * 1024
VMEM_LIMIT = V7X_VMEM_BYTES - 8 * 1024 * 1024
LANES = 128

ROW_TILE = 256
PEER_TOK_TILE = 512
PEER_EXP_TILE = 1024
PEER_I1_PER_STEP = PEER_EXP_TILE // N_KEYS

NT_DIMS = (((1,), (1,)), ((), ()))


def _params(*sem):
    return pltpu.CompilerParams(dimension_semantics=sem, vmem_limit_bytes=VMEM_LIMIT)


def _resident(shape):
    return pl.BlockSpec(shape, lambda *_: (0,) * len(shape), pipeline_mode=pl.Buffered(1))


def _rows(width, tile=ROW_TILE):
    return pl.BlockSpec((tile, width), lambda i: (i, 0))


def _rms(x, g):
    return x * lax.rsqrt(jnp.mean(x * x, axis=-1, keepdims=True) + RMS_EPS) * g


def _gmlp_in_kernel(h_ref, g_ref, w_ref, lng_ref, lnb_ref, u_ref, v_ref):
    xn = _rms(h_ref[...], g_ref[...]).astype(BF16)
    u_ref[...] = jax.nn.gelu(jnp.dot(xn, w_ref[:, :A_WIDTH], preferred_element_type=F32))
    v = jax.nn.gelu(jnp.dot(xn, w_ref[:, A_WIDTH:], preferred_element_type=F32))
    mu = jnp.mean(v, axis=-1, keepdims=True)
    vc = v - mu
    var = jnp.mean(vc * vc, axis=-1, keepdims=True)
    v_ref[...] = vc * lax.rsqrt(var + LN_EPS) * lng_ref[...] + lnb_ref[...]


def _gmlp_in(h, g, w_in, ln_g, ln_b):
    return pl.pallas_call(
        _gmlp_in_kernel,
        grid=(N_TOK // ROW_TILE,),
        in_specs=[_rows(D_MODEL), _resident((1, D_MODEL)), _resident((D_MODEL, 2 * A_WIDTH)),
                  _resident((1, A_WIDTH)), _resident((1, A_WIDTH))],
        out_specs=[_rows(A_WIDTH), _rows(A_WIDTH)],
        out_shape=[jax.ShapeDtypeStruct((N_TOK, A_WIDTH), F32)] * 2,
        compiler_params=_params("parallel"),
    )(h, g, w_in, ln_g, ln_b)


def _gmlp_out_kernel(u_ref, v_ref, h_ref, wmix_ref, bias_ref, wout_ref, o_ref, um_ref):
    for c in range(ROW_TILE // CHUNK):
        rows = slice(c * CHUNK, (c + 1) * CHUNK)
        for g in range(A_GROUPS):
            cols = slice(g * A_GROUP_DIM, (g + 1) * A_GROUP_DIM)
            mixed = jnp.dot(wmix_ref[0, g], v_ref[rows, cols].astype(BF16),
                            preferred_element_type=F32) + bias_ref[0, :, cols]
            um_ref[rows, cols] = (u_ref[rows, cols] * mixed).astype(BF16)
    o_ref[...] = h_ref[...] + jnp.dot(um_ref[...], wout_ref[...], preferred_element_type=F32)


def _gmlp_out(u, v, h, wmix, bias, w_out):
    prompt_tiles = N_PROMPT // ROW_TILE
    return pl.pallas_call(
        _gmlp_out_kernel,
        grid=(N_TOK // ROW_TILE,),
        in_specs=[_rows(A_WIDTH), _rows(A_WIDTH), _rows(D_MODEL),
                  pl.BlockSpec((1, A_GROUPS, CHUNK, CHUNK), lambda i: (i // prompt_tiles, 0, 0, 0)),
                  pl.BlockSpec((1, CHUNK, A_WIDTH), lambda i: (i // prompt_tiles, 0, 0)),
                  _resident((A_WIDTH, D_MODEL))],
        out_specs=_rows(D_MODEL),
        out_shape=jax.ShapeDtypeStruct((N_TOK, D_MODEL), F32),
        scratch_shapes=[pltpu.VMEM((ROW_TILE, A_WIDTH), BF16)],
        compiler_params=_params("parallel"),
    )(u, v, h, wmix, bias, w_out)


PEER_NTOP = PEER_TOPK + 1
TOP_SECOND = 24
TOP_ROWS = 48


def _top_values(s, top_ref, base):
    work = s
    for k in range(PEER_NTOP):
        m = jnp.max(work, axis=0, keepdims=True)
        top_ref[base + k:base + k + 1, :] = m
        work = jnp.where(work == m, -jnp.inf, work)


def _peer_route_kernel(h_ref, g_ref, wpq_ref, keys_ref, xn_ref, c1_ref, g1_ref, g2_ref, top_ref):
    xn = _rms(h_ref[...], g_ref[...]).astype(BF16)
    xn_ref[...] = xn
    q = jnp.dot(xn, wpq_ref[...], preferred_element_type=F32).astype(BF16)
    top_ref[...] = jnp.full(top_ref.shape, -jnp.inf, F32)
    for hd in range(PEER_HEADS):
        c0 = hd * 2 * PEER_HALF
        s1 = lax.dot_general(keys_ref[hd, 0], q[:, c0:c0 + PEER_HALF], NT_DIMS,
                             preferred_element_type=F32)
        s2 = lax.dot_general(keys_ref[hd, 1], q[:, c0 + PEER_HALF:c0 + 2 * PEER_HALF], NT_DIMS,
                             preferred_element_type=F32)
        _top_values(s1, top_ref, 0)
        _top_values(s2, top_ref, TOP_SECOND)
        m1 = top_ref[0:1, :]
        m2 = top_ref[TOP_SECOND:TOP_SECOND + 1, :]
        second_lo = top_ref[TOP_SECOND:TOP_SECOND + 8, :]
        cands = [m1 + second_lo, m1 + top_ref[TOP_SECOND + 8:TOP_SECOND + 16, :],
                 m1 + top_ref[TOP_SECOND + 16:TOP_SECOND + 24, :]]
        for a in range(1, PEER_NTOP):
            cands.append(top_ref[a:a + 1, :] + second_lo)
        top = m1 + m2
        zsum = jnp.zeros_like(top)
        ranked = []
        for k in range(PEER_NTOP):
            m = jnp.max(functools.reduce(jnp.maximum, cands), axis=0, keepdims=True)
            ranked.append(m)
            if k < PEER_TOPK:
                zsum = zsum + jnp.exp(m - top)
            if k + 1 < PEER_NTOP:
                cands = [jnp.where(c == m, -jnp.inf, c) for c in cands]
        thr = 0.5 * (ranked[PEER_TOPK - 1] + ranked[PEER_TOPK])
        c1_ref[hd] = jnp.exp((thr - m2) - s1)
        g1_ref[hd] = jnp.exp(s1 - m1) / zsum
        g2_ref[hd] = jnp.exp(s2 - m2)


def _peer_route(h, g, w_pq, keys):
    route_spec = pl.BlockSpec((PEER_HEADS, N_KEYS, ROW_TILE), lambda i: (0, 0, i))
    route_shape = jax.ShapeDtypeStruct((PEER_HEADS, N_KEYS, N_TOK), F32)
    return pl.pallas_call(
        _peer_route_kernel,
        grid=(N_TOK // ROW_TILE,),
        in_specs=[_rows(D_MODEL), _resident((1, D_MODEL)), _resident((D_MODEL, D_MODEL)),
                  _resident((PEER_HEADS, 2, N_KEYS, PEER_HALF))],
        out_specs=[_rows(D_MODEL)] + [route_spec] * 3,
        out_shape=[jax.ShapeDtypeStruct((N_TOK, D_MODEL), BF16)] + [route_shape] * 3,
        scratch_shapes=[pltpu.VMEM((TOP_ROWS, ROW_TILE), F32)],
        compiler_params=_params("parallel"),
    )(h, g, w_pq, keys)


PEER_TOK_STEPS = N_TOK // PEER_TOK_TILE
PEER_EXP_STEPS = N_EXPERTS // PEER_EXP_TILE
PEER_STEPS = PEER_TOK_STEPS * PEER_EXP_STEPS
PEER_LAG = 2
PEER_SUB_KEYS = 32
PEER_CHUNKS = 2


def _peer_dense_kernel(x_ref, u_ref, vt_ref, c1_ref, g1_ref, g2_ref, y_ref,
                       acc_ref, act_a_ref, act_b_ref, w_a_ref, w_b_ref):
    s = pl.program_id(0)
    contract_exp_step = jnp.maximum(s - PEER_LAG, 0) % PEER_EXP_STEPS

    @pl.when(s == 0)
    def _():
        act_b_ref[...] = jnp.zeros_like(act_b_ref)
        w_a_ref[...] = jnp.zeros_like(w_a_ref)

    @pl.when(contract_exp_step == 0)
    def _():
        acc_ref[...] = jnp.zeros_like(acc_ref)

    def step(act_new_ref, act_old_ref, w_new_ref, w_old_ref):
        act_rows = PEER_EXP_TILE // PEER_CHUNKS
        acc_rows = D_MODEL // PEER_CHUNKS
        first_keys = PEER_I1_PER_STEP // PEER_CHUNKS

        def first_key_row(ref, hd, k, ls, i):
            rows = [ref[hd, c * first_keys + k:c * first_keys + k + 1, ls] for c in range(PEER_CHUNKS)]
            row = rows[-1]
            for c in range(PEER_CHUNKS - 2, -1, -1):
                row = jnp.where(i == c, rows[c], row)
            return row

        def chunk(i, carry):
            a0 = pl.multiple_of(i * act_rows, act_rows)
            c0 = pl.multiple_of(i * acc_rows, acc_rows)
            act_new_ref[pl.ds(a0, act_rows), :] = lax.dot_general(
                u_ref[pl.ds(a0, act_rows), :], x_ref[...], NT_DIMS, preferred_element_type=F32)
            acc_ref[pl.ds(c0, acc_rows), :] += jnp.dot(vt_ref[pl.ds(c0, acc_rows), :], w_old_ref[...],
                                                       preferred_element_type=F32)
            for k in range(first_keys):
                for lt in range(PEER_TOK_TILE // LANES):
                    ls = slice(lt * LANES, (lt + 1) * LANES)
                    c1 = [first_key_row(c1_ref, hd, k, ls, i) for hd in range(PEER_HEADS)]
                    g1 = [first_key_row(g1_ref, hd, k, ls, i) for hd in range(PEER_HEADS)]
                    for k0 in range(0, N_KEYS, PEER_SUB_KEYS):
                        ks = slice(k0, k0 + PEER_SUB_KEYS)
                        rs = pl.ds(pl.multiple_of((i * first_keys + k) * N_KEYS + k0, PEER_SUB_KEYS),
                                   PEER_SUB_KEYS)
                        coef = None
                        for hd in range(PEER_HEADS):
                            g2 = g2_ref[hd, ks, ls]
                            term = jnp.where(g2 >= c1[hd], g2, 0.0) * g1[hd]
                            coef = term if coef is None else coef + term
                        w_new_ref[rs, ls] = (jax.nn.gelu(act_old_ref[rs, ls]) * coef).astype(BF16)
            return carry

        lax.fori_loop(0, PEER_CHUNKS, chunk, 0)

    @pl.when(s % 2 == 0)
    def _():
        step(act_a_ref, act_b_ref, w_b_ref, w_a_ref)

    @pl.when(s % 2 == 1)
    def _():
        step(act_b_ref, act_a_ref, w_a_ref, w_b_ref)

    @pl.when((contract_exp_step == PEER_EXP_STEPS - 1) & (s >= PEER_LAG))
    def _():
        y_ref[...] = acc_ref[...].T


def _peer_dense(xn, u_bf, vt_bf, c1, g1, g2):
    def tile(s, lag):
        return jnp.clip(s - lag, 0, PEER_STEPS - 1)

    def tok(s, lag):
        return tile(s, lag) // PEER_EXP_STEPS

    def exp(s, lag):
        return tile(s, lag) % PEER_EXP_STEPS

    first_spec = pl.BlockSpec((PEER_HEADS, PEER_I1_PER_STEP, PEER_TOK_TILE), lambda s: (0, exp(s, 1), tok(s, 1)))
    act_shape = pltpu.VMEM((PEER_EXP_TILE, PEER_TOK_TILE), F32)
    w_shape = pltpu.VMEM((PEER_EXP_TILE, PEER_TOK_TILE), BF16)
    return pl.pallas_call(
        _peer_dense_kernel,
        grid=(PEER_STEPS + PEER_LAG,),
        in_specs=[pl.BlockSpec((PEER_TOK_TILE, D_MODEL), lambda s: (tok(s, 0), 0)),
                  pl.BlockSpec((PEER_EXP_TILE, D_MODEL), lambda s: (exp(s, 0), 0)),
                  pl.BlockSpec((D_MODEL, PEER_EXP_TILE), lambda s: (0, exp(s, PEER_LAG))),
                  first_spec, first_spec,
                  pl.BlockSpec((PEER_HEADS, N_KEYS, PEER_TOK_TILE), lambda s: (0, 0, tok(s, 1)))],
        out_specs=pl.BlockSpec((PEER_TOK_TILE, D_MODEL), lambda s: (tok(s, PEER_LAG), 0)),
        out_shape=jax.ShapeDtypeStruct((N_TOK, D_MODEL), F32),
        scratch_shapes=[pltpu.VMEM((D_MODEL, PEER_TOK_TILE), F32), act_shape, act_shape, w_shape, w_shape],
        compiler_params=_params("arbitrary"),
    )(xn, u_bf, vt_bf, c1, g1, g2)


def _peer(h, g, w_pq, keys, u_tab, v_tab):
    xn, c1, g1, g2 = _peer_route(h, g, w_pq.astype(BF16), keys.astype(BF16))
    return _peer_dense(xn, u_tab.astype(BF16), v_tab.astype(BF16).T, c1, g1, g2)


def _ple_kernel(h_ref, y_ref, p_ref, g_ref, wg_ref, wp_ref, *rest, final):
    h = h_ref[...] + y_ref[...]
    gate = jax.nn.sigmoid(jnp.dot(_rms(h, g_ref[...]).astype(BF16), wg_ref[...], preferred_element_type=F32))
    out = h + gate * jnp.dot(p_ref[...].astype(BF16), wp_ref[...], preferred_element_type=F32)
    if final:
        gf_ref, o_ref, yf_ref = rest
        yf_ref[...] = _rms(out, gf_ref[...])
    else:
        (o_ref,) = rest
    o_ref[...] = out


def _ple(h, y, p, g, w_gate, w_proj, g_final=None):
    final = g_final is not None
    in_specs = [_rows(D_MODEL), _rows(D_MODEL), _rows(PLE_DIM), _resident((1, D_MODEL)),
                _resident((D_MODEL, D_MODEL)), _resident((PLE_DIM, D_MODEL))]
    args = [h, y, p, g, w_gate, w_proj]
    n_out = 1
    if final:
        in_specs.append(_resident((1, D_MODEL)))
        args.append(g_final)
        n_out = 2
    return pl.pallas_call(
        functools.partial(_ple_kernel, final=final),
        grid=(N_TOK // ROW_TILE,),
        in_specs=in_specs,
        out_specs=[_rows(D_MODEL)] * n_out,
        out_shape=[jax.ShapeDtypeStruct((N_TOK, D_MODEL), F32)] * n_out,
        compiler_params=_params("parallel"),
    )(*args)


def _proj_rope_kernel(h_ref, g_ref, w_ref, cos_ref, sin_ref, o_ref, *, n_rope_heads, width):
    xn = _rms(h_ref[...], g_ref[...]).astype(BF16)
    cos = cos_ref[...]
    sin = sin_ref[...]
    for c0 in range(0, width, KV_WIDTH):
        z = jnp.dot(xn, w_ref[:, c0:c0 + KV_WIDTH], preferred_element_type=F32)
        for hd in range(N_KV_HEADS):
            cs = slice(hd * HEAD_DIM, (hd + 1) * HEAD_DIM)
            zh = z[:, cs]
            if c0 // HEAD_DIM + hd < n_rope_heads:
                zh = zh * cos + pltpu.roll(zh, HEAD_DIM // 2, 1) * sin
            o_ref[:, c0 + hd * HEAD_DIM:c0 + (hd + 1) * HEAD_DIM] = zh


def _proj_rope(h, g, w, cos, sin, n_rope_heads):
    width = w.shape[1]
    return pl.pallas_call(
        functools.partial(_proj_rope_kernel, n_rope_heads=n_rope_heads, width=width),
        grid=(N_TOK // ROW_TILE,),
        in_specs=[_rows(D_MODEL), _resident((1, D_MODEL)), _resident((D_MODEL, width)),
                  _rows(HEAD_DIM), _rows(HEAD_DIM)],
        out_specs=_rows(width),
        out_shape=jax.ShapeDtypeStruct((N_TOK, width), F32),
        compiler_params=_params("parallel"),
    )(h, g, w, cos, sin)


def _softmax_parts(scores):
    m = functools.reduce(jnp.maximum, [jnp.max(s, axis=-1, keepdims=True) for s in scores])
    probs = [jnp.exp(s - m) for s in scores]
    z = functools.reduce(jnp.add, [jnp.sum(p, axis=-1, keepdims=True) for p in probs])
    return probs, z, m + jnp.log(z)


def _attn_prompt_kernel(q0_ref, q1_ref, q2_ref, k_ref, v_ref, o_ref, og_ref, lse_ref):
    iq = lax.broadcasted_iota(jnp.int32, (ATTN_BLOCK, 2 * ATTN_BLOCK), 0)
    ik = lax.broadcasted_iota(jnp.int32, (ATTN_BLOCK, 2 * ATTN_BLOCK), 1)
    diff = ATTN_BLOCK + iq - ik
    two_blocks = (diff >= 0) & (diff <= ATTN_BLOCK)
    own_block = (lax.broadcasted_iota(jnp.int32, (ATTN_BLOCK, ATTN_BLOCK), 0)
                 >= lax.broadcasted_iota(jnp.int32, (ATTN_BLOCK, ATTN_BLOCK), 1))
    for g, (q_ref, (_, dil)) in enumerate(zip((q0_ref, q1_ref, q2_ref), DILATED_GROUPS)):
        for res in range(dil):
            for blk in range(SEQ // dil // ATTN_BLOCK):
                first = blk * ATTN_BLOCK * dil + res
                rows = pl.ds(first, ATTN_BLOCK, stride=dil)
                if blk == 0:
                    keys, valid = rows, own_block
                else:
                    keys, valid = pl.ds(first - ATTN_BLOCK * dil, 2 * ATTN_BLOCK, stride=dil), two_blocks
                s = lax.dot_general(q_ref[rows, :].astype(BF16), k_ref[keys, :].astype(BF16), NT_DIMS,
                                    preferred_element_type=F32) * ATTN_SCALE
                (p,), z, lse = _softmax_parts([jnp.where(valid, s, -jnp.inf)])
                og_ref[g, rows, :] = jnp.dot(p.astype(BF16), v_ref[keys, :].astype(BF16),
                                             preferred_element_type=F32) / z
                lse_ref[g, rows, :] = jnp.broadcast_to(lse, (ATTN_BLOCK, HEAD_DIM))
    for row0 in range(0, SEQ, 2 * ATTN_BLOCK):
        rs = slice(row0, row0 + 2 * ATTN_BLOCK)
        lses = [lse_ref[g, rs, :] for g in range(N_DGROUPS)]
        top = functools.reduce(jnp.maximum, lses)
        wts = [jnp.exp(l - top) for l in lses]
        merged = functools.reduce(jnp.add, [w * og_ref[g, rs, :] for g, w in enumerate(wts)])
        o_ref[rs, :] = merged / functools.reduce(jnp.add, wts)


def _attn_prompt(q, kv):
    blk = (SEQ, HEAD_DIM)
    group_scratch = pltpu.VMEM((N_DGROUPS, SEQ, HEAD_DIM), F32)
    return pl.pallas_call(
        _attn_prompt_kernel,
        grid=(BATCH, N_KV_HEADS),
        in_specs=[pl.BlockSpec(blk, lambda b, h, g=g: (b, g * N_KV_HEADS + h)) for g in range(N_DGROUPS)]
        + [pl.BlockSpec(blk, lambda b, h: (b, h)), pl.BlockSpec(blk, lambda b, h: (b, N_KV_HEADS + h))],
        out_specs=pl.BlockSpec(blk, lambda b, h: (b, h)),
        out_shape=jax.ShapeDtypeStruct((N_PROMPT, KV_WIDTH), F32),
        scratch_shapes=[group_scratch, group_scratch],
        compiler_params=_params("parallel", "parallel"),
    )(q, q, q, kv, kv)


CACHE_PAGE = 16
CACHE_PAGES = PAST_LEN // CACHE_PAGE
G1_PAGES = DILATED_GROUPS[1][0] // CACHE_PAGE
G0_PAGES = DILATED_GROUPS[0][0] // CACHE_PAGE


G2_OLD_PAGES = CACHE_PAGES - G1_PAGES
QH = DEC_SEQ * N_KV_HEADS


def _sample_masks():
    row = jnp.arange(QH)[:, None]
    j, h = row // N_KV_HEADS, row % N_KV_HEADS

    def mask(n_pos, ok):
        col = jnp.arange(n_pos * N_KV_HEADS)[None, :]
        pos, hk = col // N_KV_HEADS, col % N_KV_HEADS
        return jnp.where((hk == h) & ok(pos), 0.0, -jnp.inf).astype(F32)

    n0, n1 = G0_PAGES * CACHE_PAGE, G1_PAGES * CACHE_PAGE
    past = [mask(n0, lambda c: c >= j),
            mask(n1, lambda c: (c >= j) & ((n1 + j - c) % DILATED_GROUPS[1][1] == 0)),
            mask(CACHE_PAGES * DEC_SEQ, lambda p: p % DEC_SEQ == j)]
    new = [mask(DEC_SEQ, lambda c: c <= j),
           mask(DEC_SEQ, lambda c: (c <= j) & ((j - c) % DILATED_GROUPS[1][1] == 0)),
           mask(DEC_SEQ, lambda c: c == j)]
    return past, jnp.stack(new)


def _attn_sample_kernel(q_ref, kvn_ref, ca_ref, cb_ref, m0_ref, m1_ref, m2_ref, mnew_ref, o_ref):
    def keys_values(ref, *idx):
        return tuple(ref[idx + (kv,)].reshape(-1, HEAD_DIM).astype(BF16) for kv in range(2))

    n_old = G2_OLD_PAGES * DEC_SEQ * N_KV_HEADS
    new = keys_values(kvn_ref, 0, slice(None))
    parts = [
        [(keys_values(cb_ref, 0, slice(G1_PAGES - G0_PAGES, None), slice(None)), m0_ref[...])],
        [(keys_values(cb_ref, 0, slice(None), slice(None)), m1_ref[...])],
        [(keys_values(ca_ref, 0, slice(None), slice(None)), m2_ref[:, :n_old]),
         (keys_values(cb_ref, 0, slice(None), slice(0, DEC_SEQ)), m2_ref[:, n_old:])],
    ]
    outs, lses = [], []
    for g in range(N_DGROUPS):
        q = q_ref[0, g].reshape(QH, HEAD_DIM).astype(BF16)
        group = parts[g] + [(new, mnew_ref[g])]
        scores = [lax.dot_general(q, k, NT_DIMS, preferred_element_type=F32) * ATTN_SCALE + m
                  for (k, _), m in group]
        probs, z, lse = _softmax_parts(scores)
        o = functools.reduce(jnp.add, [jnp.dot(p.astype(BF16), v, preferred_element_type=F32)
                                       for p, ((_, v), _) in zip(probs, group)])
        outs.append(o / z)
        lses.append(lse)
    top = functools.reduce(jnp.maximum, lses)
    wts = [jnp.exp(l - top) for l in lses]
    merged = functools.reduce(jnp.add, [w * o for w, o in zip(wts, outs)])
    o_ref[0] = (merged / functools.reduce(jnp.add, wts)).reshape(DEC_SEQ, N_KV_HEADS, HEAD_DIM)


def _attn_sample(q_sample, kv_sample, cache_kv):
    q5 = q_sample.reshape(DEC_BATCH, DEC_SEQ, N_DGROUPS, N_KV_HEADS, HEAD_DIM).transpose(0, 2, 1, 3, 4)
    cache = cache_kv.reshape(DEC_BATCH, CACHE_PAGES, CACHE_PAGE, 2, N_KV_HEADS, HEAD_DIM)
    (m0, m1, m2), m_new = _sample_masks()
    tile_dims = (2, N_KV_HEADS, HEAD_DIM)
    o = pl.pallas_call(
        _attn_sample_kernel,
        grid=(DEC_BATCH,),
        in_specs=[pl.BlockSpec((1, N_DGROUPS, DEC_SEQ, N_KV_HEADS, HEAD_DIM), lambda b: (b, 0, 0, 0, 0)),
                  pl.BlockSpec((1, DEC_SEQ) + tile_dims, lambda b: (b, 0, 0, 0, 0)),
                  pl.BlockSpec((1, G2_OLD_PAGES, DEC_SEQ) + tile_dims, lambda b: (b, 0, 0, 0, 0, 0)),
                  pl.BlockSpec((1, G1_PAGES, CACHE_PAGE) + tile_dims,
                               lambda b: (b, CACHE_PAGES // G1_PAGES - 1, 0, 0, 0, 0)),
                  _resident(m0.shape), _resident(m1.shape), _resident(m2.shape), _resident(m_new.shape)],
        out_specs=pl.BlockSpec((1, DEC_SEQ, N_KV_HEADS, HEAD_DIM), lambda b: (b, 0, 0, 0)),
        out_shape=jax.ShapeDtypeStruct((DEC_BATCH, DEC_SEQ, N_KV_HEADS, HEAD_DIM), F32),
        compiler_params=_params("parallel"),
    )(q5, kv_sample, cache, cache, m0, m1, m2, m_new)
    return o.reshape(N_SAMPLE, KV_WIDTH)


def _attn_out_kernel(o_ref, h_ref, wo_ref, out_ref):
    out_ref[...] = h_ref[...] + jnp.dot(o_ref[...].astype(BF16), wo_ref[...], preferred_element_type=F32)


def _attn_out(o, h, w_o):
    return pl.pallas_call(
        _attn_out_kernel,
        grid=(N_TOK // ROW_TILE,),
        in_specs=[_rows(KV_WIDTH), _rows(D_MODEL), _resident((KV_WIDTH, D_MODEL))],
        out_specs=_rows(D_MODEL),
        out_shape=jax.ShapeDtypeStruct((N_TOK, D_MODEL), F32),
        compiler_params=_params("parallel"),
    )(o, h, w_o)


def _rope_tables():
    half = HEAD_DIM // 2
    inv_freq = ROPE_THETA ** (-jnp.arange(half, dtype=F32) / half)
    pos = jnp.concatenate([jnp.tile(jnp.arange(SEQ, dtype=jnp.int32), BATCH),
                           jnp.tile(PAST_LEN + jnp.arange(DEC_SEQ, dtype=jnp.int32), DEC_BATCH)])
    ang = pos.astype(F32)[:, None] * inv_freq[None, :]
    cos, sin = jnp.cos(ang), jnp.sin(ang)
    return jnp.concatenate([cos, cos], axis=-1), jnp.concatenate([-sin, sin], axis=-1)


def _mix_weights(w_s, b_s):
    tril = jnp.tril(jnp.ones((CHUNK, CHUNK), dtype=bool))
    w_prompt = jnp.where(tril, w_s, 0.0)
    reps = CHUNK // DEC_SEQ
    small = w_prompt[:, :DEC_SEQ, :DEC_SEQ]
    eye = jnp.eye(reps, dtype=F32)
    w_sample = jnp.einsum('ab,gts->gatbs', eye, small).reshape(A_GROUPS, CHUNK, CHUNK)
    wmix = jnp.stack([w_prompt, w_sample]).astype(BF16)
    b_prompt = b_s.T
    b_sample = jnp.tile(b_s[:, :DEC_SEQ].T, (reps, 1))
    bias = jnp.repeat(jnp.stack([b_prompt, b_sample]), A_GROUP_DIM, axis=-1)
    return wmix, bias


def kernel(x_prompt, x_sample, p_prompt, p_sample, cache_kv, g_mix, a_w_in, a_ln_g, a_ln_b, a_w_s, a_b_s,
           a_w_out, g_kv, w_kv, b_w_q, b_w_o, g_ffn, peer_w_q, peer_keys, peer_u, peer_v, g_ple,
           ple_w_gate, ple_w_proj, g_final):
    h = jnp.concatenate([x_prompt.reshape(N_PROMPT, D_MODEL), x_sample.reshape(N_SAMPLE, D_MODEL)])
    p = jnp.concatenate([p_prompt.reshape(2, N_PROMPT, PLE_DIM), p_sample.reshape(2, N_SAMPLE, PLE_DIM)], axis=1)
    cos, sin = _rope_tables()
    row = lambda a: a.reshape(1, -1)

    u, v = _gmlp_in(h, row(g_mix[0]), a_w_in[0].astype(BF16), row(a_ln_g[0]), row(a_ln_b[0]))
    wmix, bias = _mix_weights(a_w_s[0], a_b_s[0])
    h = _gmlp_out(u, v, h, wmix, bias, a_w_out[0].astype(BF16))
    y = _peer(h, row(g_ffn[0]), peer_w_q[0], peer_keys[0], peer_u[0], peer_v[0])
    (h,) = _ple(h, y, p[0], row(g_ple[0]), ple_w_gate[0].astype(BF16), ple_w_proj[0].astype(BF16))

    kv = _proj_rope(h, row(g_kv), w_kv.astype(BF16), cos, sin, N_KV_HEADS)
    q = _proj_rope(h, row(g_mix[1]), b_w_q[0].astype(BF16), cos, sin, N_DGROUPS * N_KV_HEADS)

    kv_sample = kv[N_PROMPT:].reshape(DEC_BATCH, DEC_SEQ, 2, N_KV_HEADS, HEAD_DIM)
    o_sample = _attn_sample(q[N_PROMPT:].reshape(DEC_BATCH, DEC_SEQ, Q_WIDTH), kv_sample, cache_kv)
    h = _attn_out(jnp.concatenate([_attn_prompt(q, kv), o_sample]), h, b_w_o[0].astype(BF16))
    y = _peer(h, row(g_ffn[1]), peer_w_q[1], peer_keys[1], peer_u[1], peer_v[1])
    h, y_out = _ple(h, y, p[1], row(g_ple[1]), ple_w_gate[1].astype(BF16), ple_w_proj[1].astype(BF16),
                    row(g_final))

    y_prompt = y_out[:N_PROMPT].reshape(BATCH, SEQ, D_MODEL)
    y_sample = y_out[N_PROMPT:].reshape(DEC_BATCH, DEC_SEQ, D_MODEL)
    a_v_prompt = v[:N_PROMPT].reshape(BATCH, SEQ, A_WIDTH)[None, :, SEQ - CHUNK:]
    a_v_sample = v[N_PROMPT:].reshape(1, DEC_BATCH, DEC_SEQ, A_WIDTH)
    kv_prompt = kv[:N_PROMPT].reshape(BATCH, SEQ, 2, N_KV_HEADS, HEAD_DIM)
    return (y_prompt, y_sample, a_v_prompt, a_v_sample, kv_prompt, kv_sample)
```

```python
import functools

import jax
import jax.numpy as jnp
from jax import lax
from jax.experimental import pallas as pl
from jax.experimental.pallas import tpu as pltpu

F32 = jnp.float32
BF16 = jnp.bfloat16

D_MODEL = 2048
BATCH = 4
SEQ = 2048
DEC_BATCH = 128
DEC_SEQ = 8
PAST_LEN = 2048
N_PROMPT = BATCH * SEQ
N_SAMPLE = DEC_BATCH * DEC_SEQ
N_TOK = N_PROMPT + N_SAMPLE

CHUNK = 128
A_WIDTH = D_MODEL
A_GROUPS = 8
A_GROUP_DIM = A_WIDTH // A_GROUPS

HEAD_DIM = 128
N_KV_HEADS = 8
KV_WIDTH = N_KV_HEADS * HEAD_DIM
DILATED_GROUPS = ((128, 1), (512, 4), (2048, 16))
N_DGROUPS = len(DILATED_GROUPS)
Q_WIDTH = N_DGROUPS * KV_WIDTH
ATTN_BLOCK = 128
ROPE_THETA = 10000.0
ATTN_SCALE = HEAD_DIM ** -0.5

N_KEYS = 128
N_EXPERTS = N_KEYS * N_KEYS
PEER_HEADS = 8
PEER_HALF = 128
PEER_TOPK = 16
PLE_DIM = 256
RMS_EPS = 1e-6
LN_EPS = 1e-5

V7X_VMEM_BYTES = 64 * 1024 * 1024
VMEM_LIMIT = V7X_VMEM_BYTES - 8 * 1024 * 1024
LANES = 128

ROW_TILE = 256
PEER_TOK_TILE = 512
PEER_EXP_TILE = 1024
PEER_I1_PER_STEP = PEER_EXP_TILE // N_KEYS

NT_DIMS = (((1,), (1,)), ((), ()))


def _params(*sem):
    return pltpu.CompilerParams(dimension_semantics=sem, vmem_limit_bytes=VMEM_LIMIT)


def _resident(shape):
    return pl.BlockSpec(shape, lambda *_: (0,) * len(shape), pipeline_mode=pl.Buffered(1))


def _rows(width, tile=ROW_TILE):
    return pl.BlockSpec((tile, width), lambda i: (i, 0))


def _rms(x, g):
    return x * lax.rsqrt(jnp.mean(x * x, axis=-1, keepdims=True) + RMS_EPS) * g


def _gmlp_in_kernel(h_ref, g_ref, w_ref, lng_ref, lnb_ref, u_ref, v_ref):
    xn = _rms(h_ref[...], g_ref[...]).astype(BF16)
    u_ref[...] = jax.nn.gelu(jnp.dot(xn, w_ref[:, :A_WIDTH], preferred_element_type=F32))
    v = jax.nn.gelu(jnp.dot(xn, w_ref[:, A_WIDTH:], preferred_element_type=F32))
    mu = jnp.mean(v, axis=-1, keepdims=True)
    vc = v - mu
    var = jnp.mean(vc * vc, axis=-1, keepdims=True)
    v_ref[...] = vc * lax.rsqrt(var + LN_EPS) * lng_ref[...] + lnb_ref[...]


def _gmlp_in(h, g, w_in, ln_g, ln_b):
    return pl.pallas_call(
        _gmlp_in_kernel,
        grid=(N_TOK // ROW_TILE,),
        in_specs=[_rows(D_MODEL), _resident((1, D_MODEL)), _resident((D_MODEL, 2 * A_WIDTH)),
                  _resident((1, A_WIDTH)), _resident((1, A_WIDTH))],
        out_specs=[_rows(A_WIDTH), _rows(A_WIDTH)],
        out_shape=[jax.ShapeDtypeStruct((N_TOK, A_WIDTH), F32)] * 2,
        compiler_params=_params("parallel"),
    )(h, g, w_in, ln_g, ln_b)


def _gmlp_out_kernel(u_ref, v_ref, h_ref, wmix_ref, bias_ref, wout_ref, o_ref, um_ref):
    for c in range(ROW_TILE // CHUNK):
        rows = slice(c * CHUNK, (c + 1) * CHUNK)
        for g in range(A_GROUPS):
            cols = slice(g * A_GROUP_DIM, (g + 1) * A_GROUP_DIM)
            mixed = jnp.dot(wmix_ref[0, g], v_ref[rows, cols].astype(BF16),
                            preferred_element_type=F32) + bias_ref[0, :, cols]
            um_ref[rows, cols] = (u_ref[rows, cols] * mixed).astype(BF16)
    o_ref[...] = h_ref[...] + jnp.dot(um_ref[...], wout_ref[...], preferred_element_type=F32)


def _gmlp_out(u, v, h, wmix, bias, w_out):
    prompt_tiles = N_PROMPT // ROW_TILE
    return pl.pallas_call(
        _gmlp_out_kernel,
        grid=(N_TOK // ROW_TILE,),
        in_specs=[_rows(A_WIDTH), _rows(A_WIDTH), _rows(D_MODEL),
                  pl.BlockSpec((1, A_GROUPS, CHUNK, CHUNK), lambda i: (i // prompt_tiles, 0, 0, 0)),
                  pl.BlockSpec((1, CHUNK, A_WIDTH), lambda i: (i // prompt_tiles, 0, 0)),
                  _resident((A_WIDTH, D_MODEL))],
        out_specs=_rows(D_MODEL),
        out_shape=jax.ShapeDtypeStruct((N_TOK, D_MODEL), F32),
        scratch_shapes=[pltpu.VMEM((ROW_TILE, A_WIDTH), BF16)],
        compiler_params=_params("parallel"),
    )(u, v, h, wmix, bias, w_out)


PEER_NTOP = PEER_TOPK + 1
TOP_SECOND = 24
TOP_ROWS = 48


SUBLANES = 8
SORT16 = ((0, 1), (2, 3), (0, 2), (1, 3), (1, 2), (4, 5), (6, 7), (4, 6), (5, 7), (5, 6), (0, 4), (2, 6), (2, 4),
          (1, 5), (3, 7), (3, 5), (1, 2), (3, 4), (5, 6), (8, 9), (10, 11), (8, 10), (9, 11), (9, 10), (12, 13),
          (14, 15), (12, 14), (13, 15), (13, 14), (8, 12), (10, 14), (10, 12), (9, 13), (11, 15), (11, 13),
          (9, 10), (11, 12), (13, 14), (0, 8), (4, 12), (4, 8), (2, 10), (6, 14), (6, 10), (2, 4), (6, 8),
          (10, 12), (1, 9), (5, 13), (5, 9), (3, 11), (7, 15), (7, 11), (3, 5), (7, 9), (11, 13), (1, 2), (3, 4),
          (5, 6), (7, 8), (9, 10), (11, 12), (13, 14))


def _top_values(s, top_ref, base):
    lists = [s[SUBLANES * j:SUBLANES * (j + 1), :] for j in range(N_KEYS // SUBLANES)]
    for i, j in SORT16:
        lists[i], lists[j] = jnp.maximum(lists[i], lists[j]), jnp.minimum(lists[i], lists[j])
    for p in range(PEER_NTOP):
        m = jnp.max(lists[0], axis=0, keepdims=True)
        top_ref[base + p:base + p + 1, :] = m
        popped = lists[0] == m
        for d in range(min(len(lists), PEER_NTOP - 1 - p)):
            below = lists[d + 1] if d + 1 < len(lists) else -jnp.inf
            lists[d] = jnp.where(popped, below, lists[d])


def _peer_route_kernel(h_ref, g_ref, wpq_ref, keys_ref, xn_ref, c1_ref, g1_ref, g2_ref, top_ref):
    xn = _rms(h_ref[...], g_ref[...]).astype(BF16)
    xn_ref[...] = xn
    q = jnp.dot(xn, wpq_ref[...], preferred_element_type=F32).astype(BF16)
    top_ref[...] = jnp.full(top_ref.shape, -jnp.inf, F32)
    for hd in range(PEER_HEADS):
        c0 = hd * 2 * PEER_HALF
        s1 = lax.dot_general(keys_ref[hd, 0], q[:, c0:c0 + PEER_HALF], NT_DIMS,
                             preferred_element_type=F32)
        s2 = lax.dot_general(keys_ref[hd, 1], q[:, c0 + PEER_HALF:c0 + 2 * PEER_HALF], NT_DIMS,
                             preferred_element_type=F32)
        _top_values(s1, top_ref, 0)
        _top_values(s2, top_ref, TOP_SECOND)
        def first(lo, hi):
            return top_ref[lo:hi, :]

        def second(lo, hi):
            return top_ref[TOP_SECOND + lo:TOP_SECOND + hi, :]

        m1, m2 = first(0, 1), second(0, 1)
        cands = [first(0, 8) + second(b, b + 1) for b in range(3)]
        cands += [first(8, 16) + m2, first(16, 24) + m2]
        cands += [m1 + second(0, 8), m1 + second(8, 16), m1 + second(16, 24)]
        cands += [first(a, a + 1) + second(0, 8) for a in range(1, 4)]
        top = m1 + m2
        zsum = jnp.zeros_like(top)
        ranked = []
        for k in range(PEER_NTOP):
            m = jnp.max(functools.reduce(jnp.maximum, cands), axis=0, keepdims=True)
            ranked.append(m)
            if k < PEER_TOPK:
                zsum = zsum + jnp.exp(m - top)
            if k + 1 < PEER_NTOP:
                cands = [jnp.where(c == m, -jnp.inf, c) for c in cands]
        thr = 0.5 * (ranked[PEER_TOPK - 1] + ranked[PEER_TOPK])
        c1_ref[hd] = jnp.exp((thr - m2) - s1)
        g1_ref[hd] = jnp.exp(s1 - m1) / zsum
        g2_ref[hd] = jnp.exp(s2 - m2)


def _peer_route(h, g, w_pq, keys):
    route_spec = pl.BlockSpec((PEER_HEADS, N_KEYS, ROW_TILE), lambda i: (0, 0, i))
    route_shape = jax.ShapeDtypeStruct((PEER_HEADS, N_KEYS, N_TOK), F32)
    return pl.pallas_call(
        _peer_route_kernel,
        grid=(N_TOK // ROW_TILE,),
        in_specs=[_rows(D_MODEL), _resident((1, D_MODEL)), _resident((D_MODEL, D_MODEL)),
                  _resident((PEER_HEADS, 2, N_KEYS, PEER_HALF))],
        out_specs=[_rows(D_MODEL)] + [route_spec] * 3,
        out_shape=[jax.ShapeDtypeStruct((N_TOK, D_MODEL), BF16)] + [route_shape] * 3,
        scratch_shapes=[pltpu.VMEM((TOP_ROWS, ROW_TILE), F32)],
        compiler_params=_params("parallel"),
    )(h, g, w_pq, keys)


PEER_TOK_STEPS = N_TOK // PEER_TOK_TILE
PEER_EXP_STEPS = N_EXPERTS // PEER_EXP_TILE
PEER_STEPS = PEER_TOK_STEPS * PEER_EXP_STEPS
PEER_LAG = 2
PEER_SUB_KEYS = 32
PEER_CHUNKS = 2


def _peer_dense_kernel(x_ref, u_ref, vt_ref, c1_ref, g1_ref, g2_ref, y_ref,
                       acc_ref, act_a_ref, act_b_ref, w_a_ref, w_b_ref):
    s = pl.program_id(0)
    contract_exp_step = jnp.maximum(s - PEER_LAG, 0) % PEER_EXP_STEPS

    @pl.when(s == 0)
    def _():
        act_b_ref[...] = jnp.zeros_like(act_b_ref)
        w_a_ref[...] = jnp.zeros_like(w_a_ref)

    @pl.when(contract_exp_step == 0)
    def _():
        acc_ref[...] = jnp.zeros_like(acc_ref)

    def step(act_new_ref, act_old_ref, w_new_ref, w_old_ref):
        act_rows = PEER_EXP_TILE // PEER_CHUNKS
        acc_rows = D_MODEL // PEER_CHUNKS
        first_keys = PEER_I1_PER_STEP // PEER_CHUNKS

        def first_key_row(ref, hd, k, ls, i):
            rows = [ref[hd, c * first_keys + k:c * first_keys + k + 1, ls] for c in range(PEER_CHUNKS)]
            row = rows[-1]
            for c in range(PEER_CHUNKS - 2, -1, -1):
                row = jnp.where(i == c, rows[c], row)
            return row

        def chunk(i, carry):
            a0 = pl.multiple_of(i * act_rows, act_rows)
            c0 = pl.multiple_of(i * acc_rows, acc_rows)
            act_new_ref[pl.ds(a0, act_rows), :] = lax.dot_general(
                u_ref[pl.ds(a0, act_rows), :], x_ref[...], NT_DIMS, preferred_element_type=F32)
            acc_ref[pl.ds(c0, acc_rows), :] += jnp.dot(vt_ref[pl.ds(c0, acc_rows), :], w_old_ref[...],
                                                       preferred_element_type=F32)
            for k in range(first_keys):
                for lt in range(PEER_TOK_TILE // LANES):
                    ls = slice(lt * LANES, (lt + 1) * LANES)
                    c1 = [first_key_row(c1_ref, hd, k, ls, i) for hd in range(PEER_HEADS)]
                    g1 = [first_key_row(g1_ref, hd, k, ls, i) for hd in range(PEER_HEADS)]
                    for k0 in range(0, N_KEYS, PEER_SUB_KEYS):
                        ks = slice(k0, k0 + PEER_SUB_KEYS)
                        rs = pl.ds(pl.multiple_of((i * first_keys + k) * N_KEYS + k0, PEER_SUB_KEYS),
                                   PEER_SUB_KEYS)
                        coef = None
                        for hd in range(PEER_HEADS):
                            g2 = g2_ref[hd, ks, ls]
                            term = jnp.where(g2 >= c1[hd], g2, 0.0) * g1[hd]
                            coef = term if coef is None else coef + term
                        w_new_ref[rs, ls] = (jax.nn.gelu(act_old_ref[rs, ls]) * coef).astype(BF16)
            return carry

        lax.fori_loop(0, PEER_CHUNKS, chunk, 0)

    @pl.when(s % 2 == 0)
    def _():
        step(act_a_ref, act_b_ref, w_b_ref, w_a_ref)

    @pl.when(s % 2 == 1)
    def _():
        step(act_b_ref, act_a_ref, w_a_ref, w_b_ref)

    @pl.when((contract_exp_step == PEER_EXP_STEPS - 1) & (s >= PEER_LAG))
    def _():
        y_ref[...] = acc_ref[...].T


def _peer_dense(xn, u_bf, vt_bf, c1, g1, g2):
    def tile(s, lag):
        return jnp.clip(s - lag, 0, PEER_STEPS - 1)

    def tok(s, lag):
        return tile(s, lag) // PEER_EXP_STEPS

    def exp(s, lag):
        return tile(s, lag) % PEER_EXP_STEPS

    first_spec = pl.BlockSpec((PEER_HEADS, PEER_I1_PER_STEP, PEER_TOK_TILE), lambda s: (0, exp(s, 1), tok(s, 1)))
    act_shape = pltpu.VMEM((PEER_EXP_TILE, PEER_TOK_TILE), F32)
    w_shape = pltpu.VMEM((PEER_EXP_TILE, PEER_TOK_TILE), BF16)
    return pl.pallas_call(
        _peer_dense_kernel,
        grid=(PEER_STEPS + PEER_LAG,),
        in_specs=[pl.BlockSpec((PEER_TOK_TILE, D_MODEL), lambda s: (tok(s, 0), 0)),
                  pl.BlockSpec((PEER_EXP_TILE, D_MODEL), lambda s: (exp(s, 0), 0)),
                  pl.BlockSpec((D_MODEL, PEER_EXP_TILE), lambda s: (0, exp(s, PEER_LAG))),
                  first_spec, first_spec,
                  pl.BlockSpec((PEER_HEADS, N_KEYS, PEER_TOK_TILE), lambda s: (0, 0, tok(s, 1)))],
        out_specs=pl.BlockSpec((PEER_TOK_TILE, D_MODEL), lambda s: (tok(s, PEER_LAG), 0)),
        out_shape=jax.ShapeDtypeStruct((N_TOK, D_MODEL), F32),
        scratch_shapes=[pltpu.VMEM((D_MODEL, PEER_TOK_TILE), F32), act_shape, act_shape, w_shape, w_shape],
        compiler_params=_params("arbitrary"),
    )(xn, u_bf, vt_bf, c1, g1, g2)


def _peer(h, g, w_pq, keys, u_tab, v_tab):
    xn, c1, g1, g2 = _peer_route(h, g, w_pq.astype(BF16), keys.astype(BF16))
    return _peer_dense(xn, u_tab.astype(BF16), v_tab.astype(BF16).T, c1, g1, g2)


PROMPT_TILES = N_PROMPT // ROW_TILE


def _split_out(width):
    specs = [pl.BlockSpec((ROW_TILE, width), lambda i: (jnp.minimum(i, PROMPT_TILES - 1), 0)),
             pl.BlockSpec((ROW_TILE, width), lambda i: (jnp.maximum(i - PROMPT_TILES, 0), 0))]
    shapes = [jax.ShapeDtypeStruct((N_PROMPT, width), F32), jax.ShapeDtypeStruct((N_SAMPLE, width), F32)]
    return specs, shapes


def _for_split(prompt_ref, sample_ref, body):
    step = pl.program_id(0)

    @pl.when(step < PROMPT_TILES)
    def _():
        body(prompt_ref)

    @pl.when(step >= PROMPT_TILES)
    def _():
        body(sample_ref)


def _ple_kernel(h_ref, y_ref, p_ref, g_ref, wg_ref, wp_ref, *rest, final):
    h = h_ref[...] + y_ref[...]
    gate = jax.nn.sigmoid(jnp.dot(_rms(h, g_ref[...]).astype(BF16), wg_ref[...], preferred_element_type=F32))
    out = h + gate * jnp.dot(p_ref[...].astype(BF16), wp_ref[...], preferred_element_type=F32)
    if final:
        gf_ref, yp_ref, ys_ref = rest
        y_final = _rms(out, gf_ref[...])

        def store(ref):
            ref[...] = y_final

        _for_split(yp_ref, ys_ref, store)
    else:
        (o_ref,) = rest
        o_ref[...] = out


def _ple(h, y, p, g, w_gate, w_proj, g_final=None):
    final = g_final is not None
    in_specs = [_rows(D_MODEL), _rows(D_MODEL), _rows(PLE_DIM), _resident((1, D_MODEL)),
                _resident((D_MODEL, D_MODEL)), _resident((PLE_DIM, D_MODEL))]
    args = [h, y, p, g, w_gate, w_proj]
    out_specs, out_shape = [_rows(D_MODEL)], [jax.ShapeDtypeStruct((N_TOK, D_MODEL), F32)]
    if final:
        in_specs.append(_resident((1, D_MODEL)))
        args.append(g_final)
        out_specs, out_shape = _split_out(D_MODEL)
    return pl.pallas_call(
        functools.partial(_ple_kernel, final=final),
        grid=(N_TOK // ROW_TILE,),
        in_specs=in_specs,
        out_specs=out_specs,
        out_shape=out_shape,
        compiler_params=_params("arbitrary" if final else "parallel"),
    )(*args)


def _proj_rope_kernel(h_ref, g_ref, w_ref, cos_ref, sin_ref, op_ref, os_ref, *, n_rope_heads, width):
    def body(o_ref):
        xn = _rms(h_ref[...], g_ref[...]).astype(BF16)
        cos = cos_ref[...]
        sin = sin_ref[...]
        for c0 in range(0, width, KV_WIDTH):
            z = jnp.dot(xn, w_ref[:, c0:c0 + KV_WIDTH], preferred_element_type=F32)
            for hd in range(N_KV_HEADS):
                cs = slice(hd * HEAD_DIM, (hd + 1) * HEAD_DIM)
                zh = z[:, cs]
                if c0 // HEAD_DIM + hd < n_rope_heads:
                    zh = zh * cos + pltpu.roll(zh, HEAD_DIM // 2, 1) * sin
                o_ref[:, c0 + hd * HEAD_DIM:c0 + (hd + 1) * HEAD_DIM] = zh

    _for_split(op_ref, os_ref, body)


def _proj_rope(h, g, w, cos, sin, n_rope_heads):
    width = w.shape[1]
    out_specs, out_shape = _split_out(width)
    return pl.pallas_call(
        functools.partial(_proj_rope_kernel, n_rope_heads=n_rope_heads, width=width),
        grid=(N_TOK // ROW_TILE,),
        in_specs=[_rows(D_MODEL), _resident((1, D_MODEL)), _resident((D_MODEL, width)),
                  _rows(HEAD_DIM), _rows(HEAD_DIM)],
        out_specs=out_specs,
        out_shape=out_shape,
        compiler_params=_params("arbitrary"),
    )(h, g, w, cos, sin)


def _softmax_parts(scores):
    m = functools.reduce(jnp.maximum, [jnp.max(s, axis=-1, keepdims=True) for s in scores])
    probs = [jnp.exp(s - m) for s in scores]
    z = functools.reduce(jnp.add, [jnp.sum(p, axis=-1, keepdims=True) for p in probs])
    return probs, z, m + jnp.log(z)


def _attn_prompt_kernel(q0_ref, q1_ref, q2_ref, k_ref, v_ref, o_ref, og_ref, lse_ref):
    iq = lax.broadcasted_iota(jnp.int32, (ATTN_BLOCK, 2 * ATTN_BLOCK), 0)
    ik = lax.broadcasted_iota(jnp.int32, (ATTN_BLOCK, 2 * ATTN_BLOCK), 1)
    diff = ATTN_BLOCK + iq - ik
    two_blocks = (diff >= 0) & (diff <= ATTN_BLOCK)
    own_block = (lax.broadcasted_iota(jnp.int32, (ATTN_BLOCK, ATTN_BLOCK), 0)
                 >= lax.broadcasted_iota(jnp.int32, (ATTN_BLOCK, ATTN_BLOCK), 1))
    for g, (q_ref, (_, dil)) in enumerate(zip((q0_ref, q1_ref, q2_ref), DILATED_GROUPS)):
        for res in range(dil):
            for blk in range(SEQ // dil // ATTN_BLOCK):
                first = blk * ATTN_BLOCK * dil + res
                rows = pl.ds(first, ATTN_BLOCK, stride=dil)
                if blk == 0:
                    keys, valid = rows, own_block
                else:
                    keys, valid = pl.ds(first - ATTN_BLOCK * dil, 2 * ATTN_BLOCK, stride=dil), two_blocks
                s = lax.dot_general(q_ref[rows, :].astype(BF16), k_ref[keys, :].astype(BF16), NT_DIMS,
                                    preferred_element_type=F32) * ATTN_SCALE
                (p,), z, lse = _softmax_parts([jnp.where(valid, s, -jnp.inf)])
                og_ref[g, rows, :] = jnp.dot(p.astype(BF16), v_ref[keys, :].astype(BF16),
                                             preferred_element_type=F32) / z
                lse_ref[g, rows, :] = jnp.broadcast_to(lse, (ATTN_BLOCK, HEAD_DIM))
    for row0 in range(0, SEQ, 2 * ATTN_BLOCK):
        rs = slice(row0, row0 + 2 * ATTN_BLOCK)
        lses = [lse_ref[g, rs, :] for g in range(N_DGROUPS)]
        top = functools.reduce(jnp.maximum, lses)
        wts = [jnp.exp(l - top) for l in lses]
        merged = functools.reduce(jnp.add, [w * og_ref[g, rs, :] for g, w in enumerate(wts)])
        o_ref[rs, :] = merged / functools.reduce(jnp.add, wts)


def _attn_prompt(q, kv):
    blk = (SEQ, HEAD_DIM)
    group_scratch = pltpu.VMEM((N_DGROUPS, SEQ, HEAD_DIM), F32)
    return pl.pallas_call(
        _attn_prompt_kernel,
        grid=(BATCH, N_KV_HEADS),
        in_specs=[pl.BlockSpec(blk, lambda b, h, g=g: (b, g * N_KV_HEADS + h)) for g in range(N_DGROUPS)]
        + [pl.BlockSpec(blk, lambda b, h: (b, h)), pl.BlockSpec(blk, lambda b, h: (b, N_KV_HEADS + h))],
        out_specs=pl.BlockSpec(blk, lambda b, h: (b, h)),
        out_shape=jax.ShapeDtypeStruct((N_PROMPT, KV_WIDTH), F32),
        scratch_shapes=[group_scratch, group_scratch],
        compiler_params=_params("parallel", "parallel"),
    )(q, q, q, kv, kv)


CACHE_PAGE = 16
CACHE_PAGES = PAST_LEN // CACHE_PAGE
G1_PAGES = DILATED_GROUPS[1][0] // CACHE_PAGE
G0_PAGES = DILATED_GROUPS[0][0] // CACHE_PAGE


G2_OLD_PAGES = CACHE_PAGES - G1_PAGES
QH = DEC_SEQ * N_KV_HEADS


def _sample_masks():
    row = jnp.arange(QH)[:, None]
    j, h = row // N_KV_HEADS, row % N_KV_HEADS

    def mask(n_pos, ok):
        col = jnp.arange(n_pos * N_KV_HEADS)[None, :]
        pos, hk = col // N_KV_HEADS, col % N_KV_HEADS
        return jnp.where((hk == h) & ok(pos), 0.0, -jnp.inf).astype(F32)

    n0, n1 = G0_PAGES * CACHE_PAGE, G1_PAGES * CACHE_PAGE
    past = [mask(n0, lambda c: c >= j),
            mask(n1, lambda c: (c >= j) & ((n1 + j - c) % DILATED_GROUPS[1][1] == 0)),
            mask(CACHE_PAGES * DEC_SEQ, lambda p: p % DEC_SEQ == j)]
    new = [mask(DEC_SEQ, lambda c: c <= j),
           mask(DEC_SEQ, lambda c: (c <= j) & ((j - c) % DILATED_GROUPS[1][1] == 0)),
           mask(DEC_SEQ, lambda c: c == j)]
    return past, jnp.stack(new)


def _attn_sample_kernel(q_ref, kvn_ref, ca_ref, cb_ref, m0_ref, m1_ref, m2_ref, mnew_ref, o_ref):
    def keys_values(ref, *idx):
        return tuple(ref[idx + (kv,)].reshape(-1, HEAD_DIM).astype(BF16) for kv in range(2))

    n_old = G2_OLD_PAGES * DEC_SEQ * N_KV_HEADS
    new = keys_values(kvn_ref, 0, slice(None))
    parts = [
        [(keys_values(cb_ref, 0, slice(G1_PAGES - G0_PAGES, None), slice(None)), m0_ref[...])],
        [(keys_values(cb_ref, 0, slice(None), slice(None)), m1_ref[...])],
        [(keys_values(ca_ref, 0, slice(None), slice(None)), m2_ref[:, :n_old]),
         (keys_values(cb_ref, 0, slice(None), slice(0, DEC_SEQ)), m2_ref[:, n_old:])],
    ]
    outs, lses = [], []
    for g in range(N_DGROUPS):
        q = q_ref[0, g].reshape(QH, HEAD_DIM).astype(BF16)
        group = parts[g] + [(new, mnew_ref[g])]
        scores = [lax.dot_general(q, k, NT_DIMS, preferred_element_type=F32) * ATTN_SCALE + m
                  for (k, _), m in group]
        probs, z, lse = _softmax_parts(scores)
        o = functools.reduce(jnp.add, [jnp.dot(p.astype(BF16), v, preferred_element_type=F32)
                                       for p, ((_, v), _) in zip(probs, group)])
        outs.append(o / z)
        lses.append(lse)
    top = functools.reduce(jnp.maximum, lses)
    wts = [jnp.exp(l - top) for l in lses]
    merged = functools.reduce(jnp.add, [w * o for w, o in zip(wts, outs)])
    o_ref[0] = (merged / functools.reduce(jnp.add, wts)).reshape(DEC_SEQ, N_KV_HEADS, HEAD_DIM)


def _attn_sample(q_sample, kv_sample, cache_kv):
    q5 = q_sample.reshape(DEC_BATCH, DEC_SEQ, N_DGROUPS, N_KV_HEADS, HEAD_DIM).transpose(0, 2, 1, 3, 4)
    cache = cache_kv.reshape(DEC_BATCH, CACHE_PAGES, CACHE_PAGE, 2, N_KV_HEADS, HEAD_DIM)
    (m0, m1, m2), m_new = _sample_masks()
    tile_dims = (2, N_KV_HEADS, HEAD_DIM)
    o = pl.pallas_call(
        _attn_sample_kernel,
        grid=(DEC_BATCH,),
        in_specs=[pl.BlockSpec((1, N_DGROUPS, DEC_SEQ, N_KV_HEADS, HEAD_DIM), lambda b: (b, 0, 0, 0, 0)),
                  pl.BlockSpec((1, DEC_SEQ) + tile_dims, lambda b: (b, 0, 0, 0, 0)),
                  pl.BlockSpec((1, G2_OLD_PAGES, DEC_SEQ) + tile_dims, lambda b: (b, 0, 0, 0, 0, 0)),
                  pl.BlockSpec((1, G1_PAGES, CACHE_PAGE) + tile_dims,
                               lambda b: (b, CACHE_PAGES // G1_PAGES - 1, 0, 0, 0, 0)),
                  _resident(m0.shape), _resident(m1.shape), _resident(m2.shape), _resident(m_new.shape)],
        out_specs=pl.BlockSpec((1, DEC_SEQ, N_KV_HEADS, HEAD_DIM), lambda b: (b, 0, 0, 0)),
        out_shape=jax.ShapeDtypeStruct((DEC_BATCH, DEC_SEQ, N_KV_HEADS, HEAD_DIM), F32),
        compiler_params=_params("parallel"),
    )(q5, kv_sample, cache, cache, m0, m1, m2, m_new)
    return o.reshape(N_SAMPLE, KV_WIDTH)


def _attn_out_kernel(o_ref, h_ref, wo_ref, out_ref):
    out_ref[...] = h_ref[...] + jnp.dot(o_ref[...].astype(BF16), wo_ref[...], preferred_element_type=F32)


def _attn_out(o, h, w_o):
    return pl.pallas_call(
        _attn_out_kernel,
        grid=(N_TOK // ROW_TILE,),
        in_specs=[_rows(KV_WIDTH), _rows(D_MODEL), _resident((KV_WIDTH, D_MODEL))],
        out_specs=_rows(D_MODEL),
        out_shape=jax.ShapeDtypeStruct((N_TOK, D_MODEL), F32),
        compiler_params=_params("parallel"),
    )(o, h, w_o)


def _rope_tables():
    half = HEAD_DIM // 2
    inv_freq = ROPE_THETA ** (-jnp.arange(half, dtype=F32) / half)
    pos = jnp.concatenate([jnp.tile(jnp.arange(SEQ, dtype=jnp.int32), BATCH),
                           jnp.tile(PAST_LEN + jnp.arange(DEC_SEQ, dtype=jnp.int32), DEC_BATCH)])
    ang = pos.astype(F32)[:, None] * inv_freq[None, :]
    cos, sin = jnp.cos(ang), jnp.sin(ang)
    return jnp.concatenate([cos, cos], axis=-1), jnp.concatenate([-sin, sin], axis=-1)


def _mix_weights(w_s, b_s):
    tril = jnp.tril(jnp.ones((CHUNK, CHUNK), dtype=bool))
    w_prompt = jnp.where(tril, w_s, 0.0)
    reps = CHUNK // DEC_SEQ
    small = w_prompt[:, :DEC_SEQ, :DEC_SEQ]
    eye = jnp.eye(reps, dtype=F32)
    w_sample = jnp.einsum('ab,gts->gatbs', eye, small).reshape(A_GROUPS, CHUNK, CHUNK)
    wmix = jnp.stack([w_prompt, w_sample]).astype(BF16)
    b_prompt = b_s.T
    b_sample = jnp.tile(b_s[:, :DEC_SEQ].T, (reps, 1))
    bias = jnp.repeat(jnp.stack([b_prompt, b_sample]), A_GROUP_DIM, axis=-1)
    return wmix, bias


def kernel(x_prompt, x_sample, p_prompt, p_sample, cache_kv, g_mix, a_w_in, a_ln_g, a_ln_b, a_w_s, a_b_s,
           a_w_out, g_kv, w_kv, b_w_q, b_w_o, g_ffn, peer_w_q, peer_keys, peer_u, peer_v, g_ple,
           ple_w_gate, ple_w_proj, g_final):
    h = jnp.concatenate([x_prompt.reshape(N_PROMPT, D_MODEL), x_sample.reshape(N_SAMPLE, D_MODEL)])
    p = jnp.concatenate([p_prompt.reshape(2, N_PROMPT, PLE_DIM), p_sample.reshape(2, N_SAMPLE, PLE_DIM)], axis=1)
    cos, sin = _rope_tables()
    row = lambda a: a.reshape(1, -1)

    u, v = _gmlp_in(h, row(g_mix[0]), a_w_in[0].astype(BF16), row(a_ln_g[0]), row(a_ln_b[0]))
    wmix, bias = _mix_weights(a_w_s[0], a_b_s[0])
    h = _gmlp_out(u, v, h, wmix, bias, a_w_out[0].astype(BF16))
    y = _peer(h, row(g_ffn[0]), peer_w_q[0], peer_keys[0], peer_u[0], peer_v[0])
    (h,) = _ple(h, y, p[0], row(g_ple[0]), ple_w_gate[0].astype(BF16), ple_w_proj[0].astype(BF16))

    kv_prompt, kv_sample = _proj_rope(h, row(g_kv), w_kv.astype(BF16), cos, sin, N_KV_HEADS)
    q_prompt, q_sample = _proj_rope(h, row(g_mix[1]), b_w_q[0].astype(BF16), cos, sin, N_DGROUPS * N_KV_HEADS)

    kv_sample = kv_sample.reshape(DEC_BATCH, DEC_SEQ, 2, N_KV_HEADS, HEAD_DIM)
    o_sample = _attn_sample(q_sample.reshape(DEC_BATCH, DEC_SEQ, Q_WIDTH), kv_sample, cache_kv)
    h = _attn_out(jnp.concatenate([_attn_prompt(q_prompt, kv_prompt), o_sample]), h, b_w_o[0].astype(BF16))
    y = _peer(h, row(g_ffn[1]), peer_w_q[1], peer_keys[1], peer_u[1], peer_v[1])
    y_prompt, y_sample = _ple(h, y, p[1], row(g_ple[1]), ple_w_gate[1].astype(BF16),
                              ple_w_proj[1].astype(BF16), row(g_final))

    y_prompt = y_prompt.reshape(BATCH, SEQ, D_MODEL)
    y_sample = y_sample.reshape(DEC_BATCH, DEC_SEQ, D_MODEL)
    a_v_prompt = v[:N_PROMPT].reshape(BATCH, SEQ, A_WIDTH)[None, :, SEQ - CHUNK:]
    a_v_sample = v[N_PROMPT:].reshape(1, DEC_BATCH, DEC_SEQ, A_WIDTH)
    kv_prompt = kv_prompt.reshape(BATCH, SEQ, 2, N_KV_HEADS, HEAD_DIM)
    return (y_prompt, y_sample, a_v_prompt, a_v_sample, kv_prompt, kv_sample)
```

```python
import functools

import jax
import jax.numpy as jnp
from jax import lax
from jax.experimental import pallas as pl
from jax.experimental.pallas import tpu as pltpu

F32 = jnp.float32
BF16 = jnp.bfloat16

D_MODEL = 2048
BATCH = 4
SEQ = 2048
DEC_BATCH = 128
DEC_SEQ = 8
PAST_LEN = 2048
N_PROMPT = BATCH * SEQ
N_SAMPLE = DEC_BATCH * DEC_SEQ
N_TOK = N_PROMPT + N_SAMPLE

CHUNK = 128
A_WIDTH = D_MODEL
A_GROUPS = 8
A_GROUP_DIM = A_WIDTH // A_GROUPS

HEAD_DIM = 128
N_KV_HEADS = 8
KV_WIDTH = N_KV_HEADS * HEAD_DIM
DILATED_GROUPS = ((128, 1), (512, 4), (2048, 16))
N_DGROUPS = len(DILATED_GROUPS)
Q_WIDTH = N_DGROUPS * KV_WIDTH
ATTN_BLOCK = 128
ROPE_THETA = 10000.0
ATTN_SCALE = HEAD_DIM ** -0.5

N_KEYS = 128
N_EXPERTS = N_KEYS * N_KEYS
PEER_HEADS = 8
PEER_HALF = 128
PEER_TOPK = 16
PLE_DIM = 256
RMS_EPS = 1e-6
LN_EPS = 1e-5

V7X_VMEM_BYTES = 64 * 1024 * 1024
VMEM_LIMIT = V7X_VMEM_BYTES - 8 * 1024 * 1024
LANES = 128

ROW_TILE = 256
PEER_TOK_TILE = 512
PEER_EXP_TILE = 1024
PEER_I1_PER_STEP = PEER_EXP_TILE // N_KEYS

NT_DIMS = (((1,), (1,)), ((), ()))


def _params(*sem):
    return pltpu.CompilerParams(dimension_semantics=sem, vmem_limit_bytes=VMEM_LIMIT)


def _resident(shape):
    return pl.BlockSpec(shape, lambda *_: (0,) * len(shape), pipeline_mode=pl.Buffered(1))


def _rows(width, tile=ROW_TILE):
    return pl.BlockSpec((tile, width), lambda i: (i, 0))


PROMPT_TILES = N_PROMPT // ROW_TILE


def _split_rows(width):
    return [pl.BlockSpec((ROW_TILE, width), lambda i: (jnp.minimum(i, PROMPT_TILES - 1), 0)),
            pl.BlockSpec((ROW_TILE, width), lambda i: (jnp.maximum(i - PROMPT_TILES, 0), 0))]


def _split_shapes(width):
    return [jax.ShapeDtypeStruct((N_PROMPT, width), F32), jax.ShapeDtypeStruct((N_SAMPLE, width), F32)]


def _read_split(prompt_ref, sample_ref):
    return jnp.where(pl.program_id(0) < PROMPT_TILES, prompt_ref[...], sample_ref[...])


def _for_split(prompt_ref, sample_ref, body):
    step = pl.program_id(0)

    @pl.when(step < PROMPT_TILES)
    def _():
        body(prompt_ref)

    @pl.when(step >= PROMPT_TILES)
    def _():
        body(sample_ref)


def _rms(x, g):
    return x * lax.rsqrt(jnp.mean(x * x, axis=-1, keepdims=True) + RMS_EPS) * g


def _gmlp_in_kernel(xp_ref, xs_ref, g_ref, w_ref, lng_ref, lnb_ref, u_ref, v_ref):
    xn = _rms(_read_split(xp_ref, xs_ref), g_ref[...]).astype(BF16)
    u_ref[...] = jax.nn.gelu(jnp.dot(xn, w_ref[:, :A_WIDTH], preferred_element_type=F32))
    v = jax.nn.gelu(jnp.dot(xn, w_ref[:, A_WIDTH:], preferred_element_type=F32))
    mu = jnp.mean(v, axis=-1, keepdims=True)
    vc = v - mu
    var = jnp.mean(vc * vc, axis=-1, keepdims=True)
    v_ref[...] = vc * lax.rsqrt(var + LN_EPS) * lng_ref[...] + lnb_ref[...]


def _gmlp_in(x_prompt, x_sample, g, w_in, ln_g, ln_b):
    return pl.pallas_call(
        _gmlp_in_kernel,
        grid=(N_TOK // ROW_TILE,),
        in_specs=_split_rows(D_MODEL) + [_resident((1, D_MODEL)), _resident((D_MODEL, 2 * A_WIDTH)),
                                         _resident((1, A_WIDTH)), _resident((1, A_WIDTH))],
        out_specs=[_rows(A_WIDTH), _rows(A_WIDTH)],
        out_shape=[jax.ShapeDtypeStruct((N_TOK, A_WIDTH), F32)] * 2,
        compiler_params=_params("parallel"),
    )(x_prompt, x_sample, g, w_in, ln_g, ln_b)


def _gmlp_out_kernel(u_ref, v_ref, xp_ref, xs_ref, wmix_ref, bias_ref, wout_ref, o_ref, um_ref):
    for c in range(ROW_TILE // CHUNK):
        rows = slice(c * CHUNK, (c + 1) * CHUNK)
        for g in range(A_GROUPS):
            cols = slice(g * A_GROUP_DIM, (g + 1) * A_GROUP_DIM)
            mixed = jnp.dot(wmix_ref[0, g], v_ref[rows, cols].astype(BF16),
                            preferred_element_type=F32) + bias_ref[0, :, cols]
            um_ref[rows, cols] = (u_ref[rows, cols] * mixed).astype(BF16)
    o_ref[...] = _read_split(xp_ref, xs_ref) + jnp.dot(um_ref[...], wout_ref[...], preferred_element_type=F32)


def _gmlp_out(u, v, x_prompt, x_sample, wmix, bias, w_out):
    return pl.pallas_call(
        _gmlp_out_kernel,
        grid=(N_TOK // ROW_TILE,),
        in_specs=[_rows(A_WIDTH), _rows(A_WIDTH)] + _split_rows(D_MODEL)
        + [pl.BlockSpec((1, A_GROUPS, CHUNK, CHUNK), lambda i: (i // PROMPT_TILES, 0, 0, 0)),
           pl.BlockSpec((1, CHUNK, A_WIDTH), lambda i: (i // PROMPT_TILES, 0, 0)),
           _resident((A_WIDTH, D_MODEL))],
        out_specs=_rows(D_MODEL),
        out_shape=jax.ShapeDtypeStruct((N_TOK, D_MODEL), F32),
        scratch_shapes=[pltpu.VMEM((ROW_TILE, A_WIDTH), BF16)],
        compiler_params=_params("parallel"),
    )(u, v, x_prompt, x_sample, wmix, bias, w_out)


PEER_NTOP = PEER_TOPK + 1
TOP_SECOND = 24
TOP_ROWS = 48


SUBLANES = 8
SORT16 = ((0, 1), (2, 3), (0, 2), (1, 3), (1, 2), (4, 5), (6, 7), (4, 6), (5, 7), (5, 6), (0, 4), (2, 6), (2, 4),
          (1, 5), (3, 7), (3, 5), (1, 2), (3, 4), (5, 6), (8, 9), (10, 11), (8, 10), (9, 11), (9, 10), (12, 13),
          (14, 15), (12, 14), (13, 15), (13, 14), (8, 12), (10, 14), (10, 12), (9, 13), (11, 15), (11, 13),
          (9, 10), (11, 12), (13, 14), (0, 8), (4, 12), (4, 8), (2, 10), (6, 14), (6, 10), (2, 4), (6, 8),
          (10, 12), (1, 9), (5, 13), (5, 9), (3, 11), (7, 15), (7, 11), (3, 5), (7, 9), (11, 13), (1, 2), (3, 4),
          (5, 6), (7, 8), (9, 10), (11, 12), (13, 14))


def _top_values(s, top_ref, base):
    lists = [s[SUBLANES * j:SUBLANES * (j + 1), :] for j in range(N_KEYS // SUBLANES)]
    for i, j in SORT16:
        lists[i], lists[j] = jnp.maximum(lists[i], lists[j]), jnp.minimum(lists[i], lists[j])
    for p in range(PEER_NTOP):
        m = jnp.max(lists[0], axis=0, keepdims=True)
        top_ref[base + p:base + p + 1, :] = m
        popped = lists[0] == m
        for d in range(min(len(lists), PEER_NTOP - 1 - p)):
            below = lists[d + 1] if d + 1 < len(lists) else -jnp.inf
            lists[d] = jnp.where(popped, below, lists[d])


def _peer_route_kernel(h_ref, g_ref, wpq_ref, keys_ref, xn_ref, c1_ref, g1_ref, g2_ref, top_ref):
    xn = _rms(h_ref[...], g_ref[...]).astype(BF16)
    xn_ref[...] = xn
    q = jnp.dot(xn, wpq_ref[...], preferred_element_type=F32).astype(BF16)
    top_ref[...] = jnp.full(top_ref.shape, -jnp.inf, F32)
    for hd in range(PEER_HEADS):
        c0 = hd * 2 * PEER_HALF
        s1 = lax.dot_general(keys_ref[hd, 0], q[:, c0:c0 + PEER_HALF], NT_DIMS,
                             preferred_element_type=F32)
        s2 = lax.dot_general(keys_ref[hd, 1], q[:, c0 + PEER_HALF:c0 + 2 * PEER_HALF], NT_DIMS,
                             preferred_element_type=F32)
        _top_values(s1, top_ref, 0)
        _top_values(s2, top_ref, TOP_SECOND)
        def first(lo, hi):
            return top_ref[lo:hi, :]

        def second(lo, hi):
            return top_ref[TOP_SECOND + lo:TOP_SECOND + hi, :]

        m1, m2 = first(0, 1), second(0, 1)
        cands = [first(0, 8) + second(b, b + 1) for b in range(3)]
        cands += [first(8, 16) + m2, first(16, 24) + m2]
        cands += [m1 + second(0, 8), m1 + second(8, 16), m1 + second(16, 24)]
        cands += [first(a, a + 1) + second(0, 8) for a in range(1, 4)]
        top = m1 + m2
        zsum = jnp.zeros_like(top)
        ranked = []
        for k in range(PEER_NTOP):
            m = jnp.max(functools.reduce(jnp.maximum, cands), axis=0, keepdims=True)
            ranked.append(m)
            if k < PEER_TOPK:
                zsum = zsum + jnp.exp(m - top)
            if k + 1 < PEER_NTOP:
                cands = [jnp.where(c == m, -jnp.inf, c) for c in cands]
        thr = 0.5 * (ranked[PEER_TOPK - 1] + ranked[PEER_TOPK])
        c1_ref[hd] = jnp.exp((thr - m2) - s1)
        g1_ref[hd] = jnp.exp(s1 - m1) / zsum
        g2_ref[hd] = jnp.exp(s2 - m2)


def _peer_route(h, g, w_pq, keys):
    route_spec = pl.BlockSpec((PEER_HEADS, N_KEYS, ROW_TILE), lambda i: (0, 0, i))
    route_shape = jax.ShapeDtypeStruct((PEER_HEADS, N_KEYS, N_TOK), F32)
    return pl.pallas_call(
        _peer_route_kernel,
        grid=(N_TOK // ROW_TILE,),
        in_specs=[_rows(D_MODEL), _resident((1, D_MODEL)), _resident((D_MODEL, D_MODEL)),
                  _resident((PEER_HEADS, 2, N_KEYS, PEER_HALF))],
        out_specs=[_rows(D_MODEL)] + [route_spec] * 3,
        out_shape=[jax.ShapeDtypeStruct((N_TOK, D_MODEL), BF16)] + [route_shape] * 3,
        scratch_shapes=[pltpu.VMEM((TOP_ROWS, ROW_TILE), F32)],
        compiler_params=_params("parallel"),
    )(h, g, w_pq, keys)


PEER_TOK_STEPS = N_TOK // PEER_TOK_TILE
PEER_EXP_STEPS = N_EXPERTS // PEER_EXP_TILE
PEER_STEPS = PEER_TOK_STEPS * PEER_EXP_STEPS
PEER_LAG = 2
PEER_SUB_KEYS = 32
PEER_CHUNKS = 2


def _peer_dense_kernel(x_ref, u_ref, vt_ref, c1_ref, g1_ref, g2_ref, y_ref,
                       acc_ref, act_a_ref, act_b_ref, w_a_ref, w_b_ref):
    s = pl.program_id(0)
    contract_exp_step = jnp.maximum(s - PEER_LAG, 0) % PEER_EXP_STEPS

    @pl.when(s == 0)
    def _():
        act_b_ref[...] = jnp.zeros_like(act_b_ref)
        w_a_ref[...] = jnp.zeros_like(w_a_ref)

    @pl.when(contract_exp_step == 0)
    def _():
        acc_ref[...] = jnp.zeros_like(acc_ref)

    def step(act_new_ref, act_old_ref, w_new_ref, w_old_ref):
        act_rows = PEER_EXP_TILE // PEER_CHUNKS
        acc_rows = D_MODEL // PEER_CHUNKS
        first_keys = PEER_I1_PER_STEP // PEER_CHUNKS

        def first_key_row(ref, hd, k, ls, i):
            rows = [ref[hd, c * first_keys + k:c * first_keys + k + 1, ls] for c in range(PEER_CHUNKS)]
            row = rows[-1]
            for c in range(PEER_CHUNKS - 2, -1, -1):
                row = jnp.where(i == c, rows[c], row)
            return row

        def chunk(i, carry):
            a0 = pl.multiple_of(i * act_rows, act_rows)
            c0 = pl.multiple_of(i * acc_rows, acc_rows)
            act_new_ref[pl.ds(a0, act_rows), :] = lax.dot_general(
                u_ref[0, pl.ds(a0, act_rows), :], x_ref[...], NT_DIMS, preferred_element_type=F32)
            acc_ref[pl.ds(c0, acc_rows), :] += jnp.dot(vt_ref[0, pl.ds(c0, acc_rows), :], w_old_ref[...],
                                                       preferred_element_type=F32)
            for k in range(first_keys):
                for lt in range(PEER_TOK_TILE // LANES):
                    ls = slice(lt * LANES, (lt + 1) * LANES)
                    c1 = [first_key_row(c1_ref, hd, k, ls, i) for hd in range(PEER_HEADS)]
                    g1 = [first_key_row(g1_ref, hd, k, ls, i) for hd in range(PEER_HEADS)]
                    for k0 in range(0, N_KEYS, PEER_SUB_KEYS):
                        ks = slice(k0, k0 + PEER_SUB_KEYS)
                        rs = pl.ds(pl.multiple_of((i * first_keys + k) * N_KEYS + k0, PEER_SUB_KEYS),
                                   PEER_SUB_KEYS)
                        coef = None
                        for hd in range(PEER_HEADS):
                            g2 = g2_ref[hd, ks, ls]
                            term = jnp.where(g2 >= c1[hd], g2, 0.0) * g1[hd]
                            coef = term if coef is None else coef + term
                        w_new_ref[rs, ls] = (jax.nn.gelu(act_old_ref[rs, ls]) * coef).astype(BF16)
            return carry

        lax.fori_loop(0, PEER_CHUNKS, chunk, 0)

    @pl.when(s % 2 == 0)
    def _():
        step(act_a_ref, act_b_ref, w_b_ref, w_a_ref)

    @pl.when(s % 2 == 1)
    def _():
        step(act_b_ref, act_a_ref, w_a_ref, w_b_ref)

    @pl.when((contract_exp_step == PEER_EXP_STEPS - 1) & (s >= PEER_LAG))
    def _():
        y_ref[...] = acc_ref[...].T


def _peer_dense(xn, u_bf, vt_bf, layer, c1, g1, g2):
    def tile(s, lag):
        return jnp.clip(s - lag, 0, PEER_STEPS - 1)

    def tok(s, lag):
        return tile(s, lag) // PEER_EXP_STEPS

    def exp(s, lag):
        return tile(s, lag) % PEER_EXP_STEPS

    first_spec = pl.BlockSpec((PEER_HEADS, PEER_I1_PER_STEP, PEER_TOK_TILE), lambda s: (0, exp(s, 1), tok(s, 1)))
    act_shape = pltpu.VMEM((PEER_EXP_TILE, PEER_TOK_TILE), F32)
    w_shape = pltpu.VMEM((PEER_EXP_TILE, PEER_TOK_TILE), BF16)
    return pl.pallas_call(
        _peer_dense_kernel,
        grid=(PEER_STEPS + PEER_LAG,),
        in_specs=[pl.BlockSpec((PEER_TOK_TILE, D_MODEL), lambda s: (tok(s, 0), 0)),
                  pl.BlockSpec((1, PEER_EXP_TILE, D_MODEL), lambda s: (layer, exp(s, 0), 0)),
                  pl.BlockSpec((1, D_MODEL, PEER_EXP_TILE), lambda s: (layer, 0, exp(s, PEER_LAG))),
                  first_spec, first_spec,
                  pl.BlockSpec((PEER_HEADS, N_KEYS, PEER_TOK_TILE), lambda s: (0, 0, tok(s, 1)))],
        out_specs=pl.BlockSpec((PEER_TOK_TILE, D_MODEL), lambda s: (tok(s, PEER_LAG), 0)),
        out_shape=jax.ShapeDtypeStruct((N_TOK, D_MODEL), F32),
        scratch_shapes=[pltpu.VMEM((D_MODEL, PEER_TOK_TILE), F32), act_shape, act_shape, w_shape, w_shape],
        compiler_params=_params("arbitrary"),
    )(xn, u_bf, vt_bf, c1, g1, g2)


TABLE_TILE = 512


def _peer_tables_kernel(u_ref, v_ref, ub_ref, vt_ref):
    ub_ref[0] = u_ref[0].astype(BF16)
    vt_ref[0] = v_ref[0].T.astype(BF16)


def _peer_tables(peer_u, peer_v):
    n_layers = peer_u.shape[0]
    rows = pl.BlockSpec((1, TABLE_TILE, D_MODEL), lambda l, j: (l, j, 0))
    return pl.pallas_call(
        _peer_tables_kernel,
        grid=(n_layers, N_EXPERTS // TABLE_TILE),
        in_specs=[rows, rows],
        out_specs=[rows, pl.BlockSpec((1, D_MODEL, TABLE_TILE), lambda l, j: (l, 0, j))],
        out_shape=[jax.ShapeDtypeStruct((n_layers, N_EXPERTS, D_MODEL), BF16),
                   jax.ShapeDtypeStruct((n_layers, D_MODEL, N_EXPERTS), BF16)],
        compiler_params=_params("parallel", "parallel"),
    )(peer_u, peer_v)


def _peer(h, g, w_pq, keys, u_bf, vt_bf, layer):
    xn, c1, g1, g2 = _peer_route(h, g, w_pq.astype(BF16), keys.astype(BF16))
    return _peer_dense(xn, u_bf, vt_bf, layer, c1, g1, g2)


def _ple_kernel(h_ref, y_ref, p_ref, g_ref, wg_ref, wp_ref, *rest, final):
    h = h_ref[...] + y_ref[...]
    gate = jax.nn.sigmoid(jnp.dot(_rms(h, g_ref[...]).astype(BF16), wg_ref[...], preferred_element_type=F32))
    out = h + gate * jnp.dot(p_ref[...].astype(BF16), wp_ref[...], preferred_element_type=F32)
    if final:
        gf_ref, yp_ref, ys_ref = rest
        y_final = _rms(out, gf_ref[...])

        def store(ref):
            ref[...] = y_final

        _for_split(yp_ref, ys_ref, store)
    else:
        (o_ref,) = rest
        o_ref[...] = out


def _ple(h, y, p, g, w_gate, w_proj, g_final=None):
    final = g_final is not None
    in_specs = [_rows(D_MODEL), _rows(D_MODEL), _rows(PLE_DIM), _resident((1, D_MODEL)),
                _resident((D_MODEL, D_MODEL)), _resident((PLE_DIM, D_MODEL))]
    args = [h, y, p, g, w_gate, w_proj]
    out_specs, out_shape = [_rows(D_MODEL)], [jax.ShapeDtypeStruct((N_TOK, D_MODEL), F32)]
    if final:
        in_specs.append(_resident((1, D_MODEL)))
        args.append(g_final)
        out_specs, out_shape = _split_rows(D_MODEL), _split_shapes(D_MODEL)
    return pl.pallas_call(
        functools.partial(_ple_kernel, final=final),
        grid=(N_TOK // ROW_TILE,),
        in_specs=in_specs,
        out_specs=out_specs,
        out_shape=out_shape,
        compiler_params=_params("arbitrary" if final else "parallel"),
    )(*args)


def _proj_rope_kernel(h_ref, g_ref, w_ref, cos_ref, sin_ref, op_ref, os_ref, *, n_rope_heads, width):
    def body(o_ref):
        xn = _rms(h_ref[...], g_ref[...]).astype(BF16)
        cos = cos_ref[...]
        sin = sin_ref[...]
        for c0 in range(0, width, KV_WIDTH):
            z = jnp.dot(xn, w_ref[:, c0:c0 + KV_WIDTH], preferred_element_type=F32)
            for hd in range(N_KV_HEADS):
                cs = slice(hd * HEAD_DIM, (hd + 1) * HEAD_DIM)
                zh = z[:, cs]
                if c0 // HEAD_DIM + hd < n_rope_heads:
                    zh = zh * cos + pltpu.roll(zh, HEAD_DIM // 2, 1) * sin
                o_ref[:, c0 + hd * HEAD_DIM:c0 + (hd + 1) * HEAD_DIM] = zh

    _for_split(op_ref, os_ref, body)


def _proj_rope(h, g, w, cos, sin, n_rope_heads):
    width = w.shape[1]
    return pl.pallas_call(
        functools.partial(_proj_rope_kernel, n_rope_heads=n_rope_heads, width=width),
        grid=(N_TOK // ROW_TILE,),
        in_specs=[_rows(D_MODEL), _resident((1, D_MODEL)), _resident((D_MODEL, width)),
                  _rows(HEAD_DIM), _rows(HEAD_DIM)],
        out_specs=_split_rows(width),
        out_shape=_split_shapes(width),
        compiler_params=_params("arbitrary"),
    )(h, g, w, cos, sin)


def _softmax_parts(scores):
    m = functools.reduce(jnp.maximum, [jnp.max(s, axis=-1, keepdims=True) for s in scores])
    probs = [jnp.exp(s - m) for s in scores]
    z = functools.reduce(jnp.add, [jnp.sum(p, axis=-1, keepdims=True) for p in probs])
    return probs, z, m + jnp.log(z)


def _attn_prompt_kernel(q0_ref, q1_ref, q2_ref, k_ref, v_ref, o_ref, og_ref, lse_ref):
    iq = lax.broadcasted_iota(jnp.int32, (ATTN_BLOCK, 2 * ATTN_BLOCK), 0)
    ik = lax.broadcasted_iota(jnp.int32, (ATTN_BLOCK, 2 * ATTN_BLOCK), 1)
    diff = ATTN_BLOCK + iq - ik
    two_blocks = (diff >= 0) & (diff <= ATTN_BLOCK)
    own_block = (lax.broadcasted_iota(jnp.int32, (ATTN_BLOCK, ATTN_BLOCK), 0)
                 >= lax.broadcasted_iota(jnp.int32, (ATTN_BLOCK, ATTN_BLOCK), 1))
    for g, (q_ref, (_, dil)) in enumerate(zip((q0_ref, q1_ref, q2_ref), DILATED_GROUPS)):
        for res in range(dil):
            for blk in range(SEQ // dil // ATTN_BLOCK):
                first = blk * ATTN_BLOCK * dil + res
                rows = pl.ds(first, ATTN_BLOCK, stride=dil)
                if blk == 0:
                    keys, valid = rows, own_block
                else:
                    keys, valid = pl.ds(first - ATTN_BLOCK * dil, 2 * ATTN_BLOCK, stride=dil), two_blocks
                s = lax.dot_general(q_ref[rows, :].astype(BF16), k_ref[keys, :].astype(BF16), NT_DIMS,
                                    preferred_element_type=F32) * ATTN_SCALE
                (p,), z, lse = _softmax_parts([jnp.where(valid, s, -jnp.inf)])
                og_ref[g, rows, :] = jnp.dot(p.astype(BF16), v_ref[keys, :].astype(BF16),
                                             preferred_element_type=F32) / z
                lse_ref[g, rows, :] = jnp.broadcast_to(lse, (ATTN_BLOCK, HEAD_DIM))
    for row0 in range(0, SEQ, 2 * ATTN_BLOCK):
        rs = slice(row0, row0 + 2 * ATTN_BLOCK)
        lses = [lse_ref[g, rs, :] for g in range(N_DGROUPS)]
        top = functools.reduce(jnp.maximum, lses)
        wts = [jnp.exp(l - top) for l in lses]
        merged = functools.reduce(jnp.add, [w * og_ref[g, rs, :] for g, w in enumerate(wts)])
        o_ref[rs, :] = merged / functools.reduce(jnp.add, wts)


def _attn_prompt(q, kv):
    blk = (SEQ, HEAD_DIM)
    group_scratch = pltpu.VMEM((N_DGROUPS, SEQ, HEAD_DIM), F32)
    return pl.pallas_call(
        _attn_prompt_kernel,
        grid=(BATCH, N_KV_HEADS),
        in_specs=[pl.BlockSpec(blk, lambda b, h, g=g: (b, g * N_KV_HEADS + h)) for g in range(N_DGROUPS)]
        + [pl.BlockSpec(blk, lambda b, h: (b, h)), pl.BlockSpec(blk, lambda b, h: (b, N_KV_HEADS + h))],
        out_specs=pl.BlockSpec(blk, lambda b, h: (b, h)),
        out_shape=jax.ShapeDtypeStruct((N_PROMPT, KV_WIDTH), F32),
        scratch_shapes=[group_scratch, group_scratch],
        compiler_params=_params("parallel", "parallel"),
    )(q, q, q, kv, kv)


CACHE_PAGE = 16
CACHE_PAGES = PAST_LEN // CACHE_PAGE
G1_PAGES = DILATED_GROUPS[1][0] // CACHE_PAGE
G0_PAGES = DILATED_GROUPS[0][0] // CACHE_PAGE


G2_OLD_PAGES = CACHE_PAGES - G1_PAGES
QH = DEC_SEQ * N_KV_HEADS


def _sample_masks():
    row = jnp.arange(QH)[:, None]
    j, h = row // N_KV_HEADS, row % N_KV_HEADS

    def mask(n_pos, ok):
        col = jnp.arange(n_pos * N_KV_HEADS)[None, :]
        pos, hk = col // N_KV_HEADS, col % N_KV_HEADS
        return jnp.where((hk == h) & ok(pos), 0.0, -jnp.inf).astype(F32)

    n0, n1 = G0_PAGES * CACHE_PAGE, G1_PAGES * CACHE_PAGE
    past = [mask(n0, lambda c: c >= j),
            mask(n1, lambda c: (c >= j) & ((n1 + j - c) % DILATED_GROUPS[1][1] == 0)),
            mask(CACHE_PAGES * DEC_SEQ, lambda p: p % DEC_SEQ == j)]
    new = [mask(DEC_SEQ, lambda c: c <= j),
           mask(DEC_SEQ, lambda c: (c <= j) & ((j - c) % DILATED_GROUPS[1][1] == 0)),
           mask(DEC_SEQ, lambda c: c == j)]
    return past, jnp.stack(new)


def _attn_sample_kernel(q_ref, kvn_ref, ca_ref, cb_ref, m0_ref, m1_ref, m2_ref, mnew_ref, o_ref):
    def keys_values(ref, *idx):
        return tuple(ref[idx + (kv,)].reshape(-1, HEAD_DIM).astype(BF16) for kv in range(2))

    n_old = G2_OLD_PAGES * DEC_SEQ * N_KV_HEADS
    new = keys_values(kvn_ref, 0, slice(None))
    parts = [
        [(keys_values(cb_ref, 0, slice(G1_PAGES - G0_PAGES, None), slice(None)), m0_ref[...])],
        [(keys_values(cb_ref, 0, slice(None), slice(None)), m1_ref[...])],
        [(keys_values(ca_ref, 0, slice(None), slice(None)), m2_ref[:, :n_old]),
         (keys_values(cb_ref, 0, slice(None), slice(0, DEC_SEQ)), m2_ref[:, n_old:])],
    ]
    outs, lses = [], []
    for g in range(N_DGROUPS):
        q = q_ref[0, g].reshape(QH, HEAD_DIM).astype(BF16)
        group = parts[g] + [(new, mnew_ref[g])]
        scores = [lax.dot_general(q, k, NT_DIMS, preferred_element_type=F32) * ATTN_SCALE + m
                  for (k, _), m in group]
        probs, z, lse = _softmax_parts(scores)
        o = functools.reduce(jnp.add, [jnp.dot(p.astype(BF16), v, preferred_element_type=F32)
                                       for p, ((_, v), _) in zip(probs, group)])
        outs.append(o / z)
        lses.append(lse)
    top = functools.reduce(jnp.maximum, lses)
    wts = [jnp.exp(l - top) for l in lses]
    merged = functools.reduce(jnp.add, [w * o for w, o in zip(wts, outs)])
    o_ref[0] = (merged / functools.reduce(jnp.add, wts)).reshape(DEC_SEQ, N_KV_HEADS, HEAD_DIM)


def _attn_sample(q_sample, kv_sample, cache_kv):
    q5 = q_sample.reshape(DEC_BATCH, DEC_SEQ, N_DGROUPS, N_KV_HEADS, HEAD_DIM).transpose(0, 2, 1, 3, 4)
    cache = cache_kv.reshape(DEC_BATCH, CACHE_PAGES, CACHE_PAGE, 2, N_KV_HEADS, HEAD_DIM)
    (m0, m1, m2), m_new = _sample_masks()
    tile_dims = (2, N_KV_HEADS, HEAD_DIM)
    o = pl.pallas_call(
        _attn_sample_kernel,
        grid=(DEC_BATCH,),
        in_specs=[pl.BlockSpec((1, N_DGROUPS, DEC_SEQ, N_KV_HEADS, HEAD_DIM), lambda b: (b, 0, 0, 0, 0)),
                  pl.BlockSpec((1, DEC_SEQ) + tile_dims, lambda b: (b, 0, 0, 0, 0)),
                  pl.BlockSpec((1, G2_OLD_PAGES, DEC_SEQ) + tile_dims, lambda b: (b, 0, 0, 0, 0, 0)),
                  pl.BlockSpec((1, G1_PAGES, CACHE_PAGE) + tile_dims,
                               lambda b: (b, CACHE_PAGES // G1_PAGES - 1, 0, 0, 0, 0)),
                  _resident(m0.shape), _resident(m1.shape), _resident(m2.shape), _resident(m_new.shape)],
        out_specs=pl.BlockSpec((1, DEC_SEQ, N_KV_HEADS, HEAD_DIM), lambda b: (b, 0, 0, 0)),
        out_shape=jax.ShapeDtypeStruct((DEC_BATCH, DEC_SEQ, N_KV_HEADS, HEAD_DIM), F32),
        compiler_params=_params("parallel"),
    )(q5, kv_sample, cache, cache, m0, m1, m2, m_new)
    return o.reshape(N_SAMPLE, KV_WIDTH)


def _attn_out_kernel(op_ref, os_ref, h_ref, wo_ref, out_ref):
    o = _read_split(op_ref, os_ref).astype(BF16)
    out_ref[...] = h_ref[...] + jnp.dot(o, wo_ref[...], preferred_element_type=F32)


def _attn_out(o_prompt, o_sample, h, w_o):
    return pl.pallas_call(
        _attn_out_kernel,
        grid=(N_TOK // ROW_TILE,),
        in_specs=_split_rows(KV_WIDTH) + [_rows(D_MODEL), _resident((KV_WIDTH, D_MODEL))],
        out_specs=_rows(D_MODEL),
        out_shape=jax.ShapeDtypeStruct((N_TOK, D_MODEL), F32),
        compiler_params=_params("parallel"),
    )(o_prompt, o_sample, h, w_o)


def _rope_tables():
    half = HEAD_DIM // 2
    inv_freq = ROPE_THETA ** (-jnp.arange(half, dtype=F32) / half)
    pos = jnp.concatenate([jnp.tile(jnp.arange(SEQ, dtype=jnp.int32), BATCH),
                           jnp.tile(PAST_LEN + jnp.arange(DEC_SEQ, dtype=jnp.int32), DEC_BATCH)])
    ang = pos.astype(F32)[:, None] * inv_freq[None, :]
    cos, sin = jnp.cos(ang), jnp.sin(ang)
    return jnp.concatenate([cos, cos], axis=-1), jnp.concatenate([-sin, sin], axis=-1)


def _mix_weights(w_s, b_s):
    tril = jnp.tril(jnp.ones((CHUNK, CHUNK), dtype=bool))
    w_prompt = jnp.where(tril, w_s, 0.0)
    reps = CHUNK // DEC_SEQ
    small = w_prompt[:, :DEC_SEQ, :DEC_SEQ]
    eye = jnp.eye(reps, dtype=F32)
    w_sample = jnp.einsum('ab,gts->gatbs', eye, small).reshape(A_GROUPS, CHUNK, CHUNK)
    wmix = jnp.stack([w_prompt, w_sample]).astype(BF16)
    b_prompt = b_s.T
    b_sample = jnp.tile(b_s[:, :DEC_SEQ].T, (reps, 1))
    bias = jnp.repeat(jnp.stack([b_prompt, b_sample]), A_GROUP_DIM, axis=-1)
    return wmix, bias


def kernel(x_prompt, x_sample, p_prompt, p_sample, cache_kv, g_mix, a_w_in, a_ln_g, a_ln_b, a_w_s, a_b_s,
           a_w_out, g_kv, w_kv, b_w_q, b_w_o, g_ffn, peer_w_q, peer_keys, peer_u, peer_v, g_ple,
           ple_w_gate, ple_w_proj, g_final):
    xp, xs = x_prompt.reshape(N_PROMPT, D_MODEL), x_sample.reshape(N_SAMPLE, D_MODEL)
    p = jnp.concatenate([p_prompt.reshape(2, N_PROMPT, PLE_DIM), p_sample.reshape(2, N_SAMPLE, PLE_DIM)], axis=1)
    cos, sin = _rope_tables()
    u_bf, vt_bf = _peer_tables(peer_u, peer_v)
    row = lambda a: a.reshape(1, -1)

    u, v = _gmlp_in(xp, xs, row(g_mix[0]), a_w_in[0].astype(BF16), row(a_ln_g[0]), row(a_ln_b[0]))
    wmix, bias = _mix_weights(a_w_s[0], a_b_s[0])
    h = _gmlp_out(u, v, xp, xs, wmix, bias, a_w_out[0].astype(BF16))
    y = _peer(h, row(g_ffn[0]), peer_w_q[0], peer_keys[0], u_bf, vt_bf, 0)
    (h,) = _ple(h, y, p[0], row(g_ple[0]), ple_w_gate[0].astype(BF16), ple_w_proj[0].astype(BF16))

    kv_prompt, kv_sample = _proj_rope(h, row(g_kv), w_kv.astype(BF16), cos, sin, N_KV_HEADS)
    q_prompt, q_sample = _proj_rope(h, row(g_mix[1]), b_w_q[0].astype(BF16), cos, sin, N_DGROUPS * N_KV_HEADS)

    kv_sample = kv_sample.reshape(DEC_BATCH, DEC_SEQ, 2, N_KV_HEADS, HEAD_DIM)
    o_sample = _attn_sample(q_sample.reshape(DEC_BATCH, DEC_SEQ, Q_WIDTH), kv_sample, cache_kv)
    h = _attn_out(_attn_prompt(q_prompt, kv_prompt), o_sample, h, b_w_o[0].astype(BF16))
    y = _peer(h, row(g_ffn[1]), peer_w_q[1], peer_keys[1], u_bf, vt_bf, 1)
    y_prompt, y_sample = _ple(h, y, p[1], row(g_ple[1]), ple_w_gate[1].astype(BF16),
                              ple_w_proj[1].astype(BF16), row(g_final))

    y_prompt = y_prompt.reshape(BATCH, SEQ, D_MODEL)
    y_sample = y_sample.reshape(DEC_BATCH, DEC_SEQ, D_MODEL)
    a_v_prompt = v[:N_PROMPT].reshape(BATCH, SEQ, A_WIDTH)[None, :, SEQ - CHUNK:]
    a_v_sample = v[N_PROMPT:].reshape(1, DEC_BATCH, DEC_SEQ, A_WIDTH)
    kv_prompt = kv_prompt.reshape(BATCH, SEQ, 2, N_KV_HEADS, HEAD_DIM)
    return (y_prompt, y_sample, a_v_prompt, a_v_sample, kv_prompt, kv_sample)
```

```python
import functools

import jax
import jax.numpy as jnp
from jax import lax
from jax.experimental import pallas as pl
from jax.experimental.pallas import tpu as pltpu

F32 = jnp.float32
BF16 = jnp.bfloat16

D_MODEL = 2048
BATCH = 4
SEQ = 2048
DEC_BATCH = 128
DEC_SEQ = 8
PAST_LEN = 2048
N_PROMPT = BATCH * SEQ
N_SAMPLE = DEC_BATCH * DEC_SEQ
N_TOK = N_PROMPT + N_SAMPLE

CHUNK = 128
A_WIDTH = D_MODEL
A_GROUPS = 8
A_GROUP_DIM = A_WIDTH // A_GROUPS

HEAD_DIM = 128
N_KV_HEADS = 8
KV_WIDTH = N_KV_HEADS * HEAD_DIM
DILATED_GROUPS = ((128, 1), (512, 4), (2048, 16))
N_DGROUPS = len(DILATED_GROUPS)
Q_WIDTH = N_DGROUPS * KV_WIDTH
ATTN_BLOCK = 128
ROPE_THETA = 10000.0
ATTN_SCALE = HEAD_DIM ** -0.5

N_KEYS = 128
N_EXPERTS = N_KEYS * N_KEYS
PEER_HEADS = 8
PEER_HALF = 128
PEER_TOPK = 16
PLE_DIM = 256
RMS_EPS = 1e-6
LN_EPS = 1e-5

V7X_VMEM_BYTES = 64 * 1024 * 1024
VMEM_LIMIT = V7X_VMEM_BYTES - 8 * 1024 * 1024
LANES = 128

ROW_TILE = 256
PEER_TOK_TILE = 512
PEER_EXP_TILE = 1024
PEER_I1_PER_STEP = PEER_EXP_TILE // N_KEYS

NT_DIMS = (((1,), (1,)), ((), ()))


def _params(*sem):
    return pltpu.CompilerParams(dimension_semantics=sem, vmem_limit_bytes=VMEM_LIMIT)


def _resident(shape):
    return pl.BlockSpec(shape, lambda *_: (0,) * len(shape), pipeline_mode=pl.Buffered(1))


def _rows(width, tile=ROW_TILE):
    return pl.BlockSpec((tile, width), lambda i: (i, 0))


PROMPT_TILES = N_PROMPT // ROW_TILE


def _split_rows(width):
    return [pl.BlockSpec((ROW_TILE, width), lambda i: (jnp.minimum(i, PROMPT_TILES - 1), 0)),
            pl.BlockSpec((ROW_TILE, width), lambda i: (jnp.maximum(i - PROMPT_TILES, 0), 0))]


def _split_shapes(width):
    return [jax.ShapeDtypeStruct((N_PROMPT, width), F32), jax.ShapeDtypeStruct((N_SAMPLE, width), F32)]


def _read_split(prompt_ref, sample_ref):
    return jnp.where(pl.program_id(0) < PROMPT_TILES, prompt_ref[...], sample_ref[...])


def _for_split(prompt_ref, sample_ref, body):
    step = pl.program_id(0)

    @pl.when(step < PROMPT_TILES)
    def _():
        body(prompt_ref)

    @pl.when(step >= PROMPT_TILES)
    def _():
        body(sample_ref)


def _rms(x, g):
    return x * lax.rsqrt(jnp.mean(x * x, axis=-1, keepdims=True) + RMS_EPS) * g


def _gmlp_in_kernel(xp_ref, xs_ref, g_ref, w_ref, lng_ref, lnb_ref, u_ref, v_ref):
    xn = _rms(_read_split(xp_ref, xs_ref), g_ref[...]).astype(BF16)
    u_ref[...] = jax.nn.gelu(jnp.dot(xn, w_ref[:, :A_WIDTH], preferred_element_type=F32))
    v = jax.nn.gelu(jnp.dot(xn, w_ref[:, A_WIDTH:], preferred_element_type=F32))
    mu = jnp.mean(v, axis=-1, keepdims=True)
    vc = v - mu
    var = jnp.mean(vc * vc, axis=-1, keepdims=True)
    v_ref[...] = vc * lax.rsqrt(var + LN_EPS) * lng_ref[...] + lnb_ref[...]


def _gmlp_in(x_prompt, x_sample, g, w_in, ln_g, ln_b):
    return pl.pallas_call(
        _gmlp_in_kernel,
        grid=(N_TOK // ROW_TILE,),
        in_specs=_split_rows(D_MODEL) + [_resident((1, D_MODEL)), _resident((D_MODEL, 2 * A_WIDTH)),
                                         _resident((1, A_WIDTH)), _resident((1, A_WIDTH))],
        out_specs=[_rows(A_WIDTH), _rows(A_WIDTH)],
        out_shape=[jax.ShapeDtypeStruct((N_TOK, A_WIDTH), F32)] * 2,
        compiler_params=_params("parallel"),
    )(x_prompt, x_sample, g, w_in, ln_g, ln_b)


def _gmlp_out_kernel(u_ref, v_ref, xp_ref, xs_ref, wmix_ref, bias_ref, wout_ref, o_ref, um_ref):
    for c in range(ROW_TILE // CHUNK):
        rows = slice(c * CHUNK, (c + 1) * CHUNK)
        for g in range(A_GROUPS):
            cols = slice(g * A_GROUP_DIM, (g + 1) * A_GROUP_DIM)
            mixed = jnp.dot(wmix_ref[0, g], v_ref[rows, cols].astype(BF16),
                            preferred_element_type=F32) + bias_ref[0, :, cols]
            um_ref[rows, cols] = (u_ref[rows, cols] * mixed).astype(BF16)
    o_ref[...] = _read_split(xp_ref, xs_ref) + jnp.dot(um_ref[...], wout_ref[...], preferred_element_type=F32)


def _gmlp_out(u, v, x_prompt, x_sample, wmix, bias, w_out):
    return pl.pallas_call(
        _gmlp_out_kernel,
        grid=(N_TOK // ROW_TILE,),
        in_specs=[_rows(A_WIDTH), _rows(A_WIDTH)] + _split_rows(D_MODEL)
        + [pl.BlockSpec((1, A_GROUPS, CHUNK, CHUNK), lambda i: (i // PROMPT_TILES, 0, 0, 0)),
           pl.BlockSpec((1, CHUNK, A_WIDTH), lambda i: (i // PROMPT_TILES, 0, 0)),
           _resident((A_WIDTH, D_MODEL))],
        out_specs=_rows(D_MODEL),
        out_shape=jax.ShapeDtypeStruct((N_TOK, D_MODEL), F32),
        scratch_shapes=[pltpu.VMEM((ROW_TILE, A_WIDTH), BF16)],
        compiler_params=_params("parallel"),
    )(u, v, x_prompt, x_sample, wmix, bias, w_out)


PEER_NTOP = PEER_TOPK + 1
TOP_SECOND = 24
TOP_ROWS = 48


SUBLANES = 8
SORT16 = ((0, 1), (2, 3), (0, 2), (1, 3), (1, 2), (4, 5), (6, 7), (4, 6), (5, 7), (5, 6), (0, 4), (2, 6), (2, 4),
          (1, 5), (3, 7), (3, 5), (1, 2), (3, 4), (5, 6), (8, 9), (10, 11), (8, 10), (9, 11), (9, 10), (12, 13),
          (14, 15), (12, 14), (13, 15), (13, 14), (8, 12), (10, 14), (10, 12), (9, 13), (11, 15), (11, 13),
          (9, 10), (11, 12), (13, 14), (0, 8), (4, 12), (4, 8), (2, 10), (6, 14), (6, 10), (2, 4), (6, 8),
          (10, 12), (1, 9), (5, 13), (5, 9), (3, 11), (7, 15), (7, 11), (3, 5), (7, 9), (11, 13), (1, 2), (3, 4),
          (5, 6), (7, 8), (9, 10), (11, 12), (13, 14))


def _top_values(s, top_ref, base):
    lists = [s[SUBLANES * j:SUBLANES * (j + 1), :] for j in range(N_KEYS // SUBLANES)]
    for i, j in SORT16:
        lists[i], lists[j] = jnp.maximum(lists[i], lists[j]), jnp.minimum(lists[i], lists[j])
    for p in range(PEER_NTOP):
        m = jnp.max(lists[0], axis=0, keepdims=True)
        top_ref[base + p:base + p + 1, :] = m
        popped = lists[0] == m
        for d in range(min(len(lists), PEER_NTOP - 1 - p)):
            below = lists[d + 1] if d + 1 < len(lists) else -jnp.inf
            lists[d] = jnp.where(popped, below, lists[d])


def _peer_route_kernel(h_ref, g_ref, wpq_ref, keys_ref, xnt_ref, c1_ref, g1_ref, g2_ref, top_ref):
    xn_f32 = _rms(h_ref[...], g_ref[...])
    xn = xn_f32.astype(BF16)
    xnt_ref[...] = xn_f32.T.astype(BF16)
    q = jnp.dot(xn, wpq_ref[...], preferred_element_type=F32).astype(BF16)
    top_ref[...] = jnp.full(top_ref.shape, -jnp.inf, F32)
    for hd in range(PEER_HEADS):
        c0 = hd * 2 * PEER_HALF
        s1 = lax.dot_general(keys_ref[hd, 0], q[:, c0:c0 + PEER_HALF], NT_DIMS,
                             preferred_element_type=F32)
        s2 = lax.dot_general(keys_ref[hd, 1], q[:, c0 + PEER_HALF:c0 + 2 * PEER_HALF], NT_DIMS,
                             preferred_element_type=F32)
        _top_values(s1, top_ref, 0)
        _top_values(s2, top_ref, TOP_SECOND)
        def first(lo, hi):
            return top_ref[lo:hi, :]

        def second(lo, hi):
            return top_ref[TOP_SECOND + lo:TOP_SECOND + hi, :]

        m1, m2 = first(0, 1), second(0, 1)
        cands = [first(0, 8) + second(b, b + 1) for b in range(3)]
        cands += [first(8, 16) + m2, first(16, 24) + m2]
        cands += [m1 + second(0, 8), m1 + second(8, 16), m1 + second(16, 24)]
        cands += [first(a, a + 1) + second(0, 8) for a in range(1, 4)]
        top = m1 + m2
        zsum = jnp.zeros_like(top)
        ranked = []
        for k in range(PEER_NTOP):
            m = jnp.max(functools.reduce(jnp.maximum, cands), axis=0, keepdims=True)
            ranked.append(m)
            if k < PEER_TOPK:
                zsum = zsum + jnp.exp(m - top)
            if k + 1 < PEER_NTOP:
                cands = [jnp.where(c == m, -jnp.inf, c) for c in cands]
        thr = 0.5 * (ranked[PEER_TOPK - 1] + ranked[PEER_TOPK])
        c1_ref[hd] = jnp.exp((thr - m2) - s1)
        g1_ref[hd] = jnp.exp(s1 - m1) / zsum
        g2_ref[hd] = jnp.exp(s2 - m2)


def _peer_route(h, g, w_pq, keys):
    route_spec = pl.BlockSpec((PEER_HEADS, N_KEYS, ROW_TILE), lambda i: (0, 0, i))
    route_shape = jax.ShapeDtypeStruct((PEER_HEADS, N_KEYS, N_TOK), F32)
    return pl.pallas_call(
        _peer_route_kernel,
        grid=(N_TOK // ROW_TILE,),
        in_specs=[_rows(D_MODEL), _resident((1, D_MODEL)), _resident((D_MODEL, D_MODEL)),
                  _resident((PEER_HEADS, 2, N_KEYS, PEER_HALF))],
        out_specs=[pl.BlockSpec((D_MODEL, ROW_TILE), lambda i: (0, i))] + [route_spec] * 3,
        out_shape=[jax.ShapeDtypeStruct((D_MODEL, N_TOK), BF16)] + [route_shape] * 3,
        scratch_shapes=[pltpu.VMEM((TOP_ROWS, ROW_TILE), F32)],
        compiler_params=_params("parallel"),
    )(h, g, w_pq, keys)


PEER_TOK_STEPS = N_TOK // PEER_TOK_TILE
PEER_EXP_STEPS = N_EXPERTS // PEER_EXP_TILE
PEER_STEPS = PEER_TOK_STEPS * PEER_EXP_STEPS
PEER_LAG = 2
PEER_SUB_KEYS = 32
PEER_CHUNKS = 2


def _peer_dense_kernel(xt_ref, u_ref, vt_ref, c1_ref, g1_ref, g2_ref, y_ref,
                       acc_ref, act_a_ref, act_b_ref, w_a_ref, w_b_ref):
    s = pl.program_id(0)
    contract_exp_step = jnp.maximum(s - PEER_LAG, 0) % PEER_EXP_STEPS

    @pl.when(s == 0)
    def _():
        act_b_ref[...] = jnp.zeros_like(act_b_ref)
        w_a_ref[...] = jnp.zeros_like(w_a_ref)

    @pl.when(contract_exp_step == 0)
    def _():
        acc_ref[...] = jnp.zeros_like(acc_ref)

    def step(act_new_ref, act_old_ref, w_new_ref, w_old_ref):
        act_rows = PEER_EXP_TILE // PEER_CHUNKS
        acc_rows = D_MODEL // PEER_CHUNKS
        first_keys = PEER_I1_PER_STEP // PEER_CHUNKS

        def first_key_row(ref, hd, k, ls, i):
            rows = [ref[hd, c * first_keys + k:c * first_keys + k + 1, ls] for c in range(PEER_CHUNKS)]
            row = rows[-1]
            for c in range(PEER_CHUNKS - 2, -1, -1):
                row = jnp.where(i == c, rows[c], row)
            return row

        def chunk(i, carry):
            a0 = pl.multiple_of(i * act_rows, act_rows)
            c0 = pl.multiple_of(i * acc_rows, acc_rows)
            act_new_ref[pl.ds(a0, act_rows), :] = jnp.dot(
                u_ref[0, pl.ds(a0, act_rows), :], xt_ref[...], preferred_element_type=F32)
            acc_ref[pl.ds(c0, acc_rows), :] += jnp.dot(vt_ref[0, pl.ds(c0, acc_rows), :], w_old_ref[...],
                                                       preferred_element_type=F32)
            for k in range(first_keys):
                for lt in range(PEER_TOK_TILE // LANES):
                    ls = slice(lt * LANES, (lt + 1) * LANES)
                    c1 = [first_key_row(c1_ref, hd, k, ls, i) for hd in range(PEER_HEADS)]
                    g1 = [first_key_row(g1_ref, hd, k, ls, i) for hd in range(PEER_HEADS)]
                    for k0 in range(0, N_KEYS, PEER_SUB_KEYS):
                        ks = slice(k0, k0 + PEER_SUB_KEYS)
                        rs = pl.ds(pl.multiple_of((i * first_keys + k) * N_KEYS + k0, PEER_SUB_KEYS),
                                   PEER_SUB_KEYS)
                        coef = None
                        for hd in range(PEER_HEADS):
                            g2 = g2_ref[hd, ks, ls]
                            term = jnp.where(g2 >= c1[hd], g2, 0.0) * g1[hd]
                            coef = term if coef is None else coef + term
                        w_new_ref[rs, ls] = (jax.nn.gelu(act_old_ref[rs, ls]) * coef).astype(BF16)
            return carry

        lax.fori_loop(0, PEER_CHUNKS, chunk, 0)

    @pl.when(s % 2 == 0)
    def _():
        step(act_a_ref, act_b_ref, w_b_ref, w_a_ref)

    @pl.when(s % 2 == 1)
    def _():
        step(act_b_ref, act_a_ref, w_a_ref, w_b_ref)

    @pl.when((contract_exp_step == PEER_EXP_STEPS - 1) & (s >= PEER_LAG))
    def _():
        y_ref[...] = acc_ref[...].T


def _peer_dense(xnt, u_bf, vt_bf, layer, c1, g1, g2):
    def tile(s, lag):
        return jnp.clip(s - lag, 0, PEER_STEPS - 1)

    def tok(s, lag):
        return tile(s, lag) // PEER_EXP_STEPS

    def exp(s, lag):
        return tile(s, lag) % PEER_EXP_STEPS

    first_spec = pl.BlockSpec((PEER_HEADS, PEER_I1_PER_STEP, PEER_TOK_TILE), lambda s: (0, exp(s, 1), tok(s, 1)))
    act_shape = pltpu.VMEM((PEER_EXP_TILE, PEER_TOK_TILE), F32)
    w_shape = pltpu.VMEM((PEER_EXP_TILE, PEER_TOK_TILE), BF16)
    return pl.pallas_call(
        _peer_dense_kernel,
        grid=(PEER_STEPS + PEER_LAG,),
        in_specs=[pl.BlockSpec((D_MODEL, PEER_TOK_TILE), lambda s: (0, tok(s, 0))),
                  pl.BlockSpec((1, PEER_EXP_TILE, D_MODEL), lambda s: (layer, exp(s, 0), 0)),
                  pl.BlockSpec((1, D_MODEL, PEER_EXP_TILE), lambda s: (layer, 0, exp(s, PEER_LAG))),
                  first_spec, first_spec,
                  pl.BlockSpec((PEER_HEADS, N_KEYS, PEER_TOK_TILE), lambda s: (0, 0, tok(s, 1)))],
        out_specs=pl.BlockSpec((PEER_TOK_TILE, D_MODEL), lambda s: (tok(s, PEER_LAG), 0)),
        out_shape=jax.ShapeDtypeStruct((N_TOK, D_MODEL), F32),
        scratch_shapes=[pltpu.VMEM((D_MODEL, PEER_TOK_TILE), F32), act_shape, act_shape, w_shape, w_shape],
        compiler_params=_params("arbitrary"),
    )(xnt, u_bf, vt_bf, c1, g1, g2)


TABLE_TILE = 512


def _peer_tables_kernel(u_ref, v_ref, ub_ref, vt_ref):
    ub_ref[0] = u_ref[0].astype(BF16)
    vt_ref[0] = v_ref[0].T.astype(BF16)


def _peer_tables(peer_u, peer_v):
    n_layers = peer_u.shape[0]
    rows = pl.BlockSpec((1, TABLE_TILE, D_MODEL), lambda l, j: (l, j, 0))
    return pl.pallas_call(
        _peer_tables_kernel,
        grid=(n_layers, N_EXPERTS // TABLE_TILE),
        in_specs=[rows, rows],
        out_specs=[rows, pl.BlockSpec((1, D_MODEL, TABLE_TILE), lambda l, j: (l, 0, j))],
        out_shape=[jax.ShapeDtypeStruct((n_layers, N_EXPERTS, D_MODEL), BF16),
                   jax.ShapeDtypeStruct((n_layers, D_MODEL, N_EXPERTS), BF16)],
        compiler_params=_params("parallel", "parallel"),
    )(peer_u, peer_v)


def _peer(h, g, w_pq, keys, u_bf, vt_bf, layer):
    xnt, c1, g1, g2 = _peer_route(h, g, w_pq.astype(BF16), keys.astype(BF16))
    return _peer_dense(xnt, u_bf, vt_bf, layer, c1, g1, g2)


def _ple_kernel(h_ref, y_ref, p_ref, g_ref, wg_ref, wp_ref, *rest, final):
    h = h_ref[...] + y_ref[...]
    gate = jax.nn.sigmoid(jnp.dot(_rms(h, g_ref[...]).astype(BF16), wg_ref[...], preferred_element_type=F32))
    out = h + gate * jnp.dot(p_ref[...].astype(BF16), wp_ref[...], preferred_element_type=F32)
    if final:
        gf_ref, yp_ref, ys_ref = rest
        y_final = _rms(out, gf_ref[...])

        def store(ref):
            ref[...] = y_final

        _for_split(yp_ref, ys_ref, store)
    else:
        (o_ref,) = rest
        o_ref[...] = out


def _ple(h, y, p, g, w_gate, w_proj, g_final=None):
    final = g_final is not None
    in_specs = [_rows(D_MODEL), _rows(D_MODEL), _rows(PLE_DIM), _resident((1, D_MODEL)),
                _resident((D_MODEL, D_MODEL)), _resident((PLE_DIM, D_MODEL))]
    args = [h, y, p, g, w_gate, w_proj]
    out_specs, out_shape = [_rows(D_MODEL)], [jax.ShapeDtypeStruct((N_TOK, D_MODEL), F32)]
    if final:
        in_specs.append(_resident((1, D_MODEL)))
        args.append(g_final)
        out_specs, out_shape = _split_rows(D_MODEL), _split_shapes(D_MODEL)
    return pl.pallas_call(
        functools.partial(_ple_kernel, final=final),
        grid=(N_TOK // ROW_TILE,),
        in_specs=in_specs,
        out_specs=out_specs,
        out_shape=out_shape,
        compiler_params=_params("arbitrary" if final else "parallel"),
    )(*args)


def _proj_rope_kernel(h_ref, g_ref, w_ref, cos_ref, sin_ref, op_ref, os_ref, *, n_rope_heads, width):
    def body(o_ref):
        xn = _rms(h_ref[...], g_ref[...]).astype(BF16)
        cos = cos_ref[...]
        sin = sin_ref[...]
        for c0 in range(0, width, KV_WIDTH):
            z = jnp.dot(xn, w_ref[:, c0:c0 + KV_WIDTH], preferred_element_type=F32)
            for hd in range(N_KV_HEADS):
                cs = slice(hd * HEAD_DIM, (hd + 1) * HEAD_DIM)
                zh = z[:, cs]
                if c0 // HEAD_DIM + hd < n_rope_heads:
                    zh = zh * cos + pltpu.roll(zh, HEAD_DIM // 2, 1) * sin
                o_ref[:, c0 + hd * HEAD_DIM:c0 + (hd + 1) * HEAD_DIM] = zh

    _for_split(op_ref, os_ref, body)


def _proj_rope(h, g, w, cos, sin, n_rope_heads):
    width = w.shape[1]
    return pl.pallas_call(
        functools.partial(_proj_rope_kernel, n_rope_heads=n_rope_heads, width=width),
        grid=(N_TOK // ROW_TILE,),
        in_specs=[_rows(D_MODEL), _resident((1, D_MODEL)), _resident((D_MODEL, width)),
                  _rows(HEAD_DIM), _rows(HEAD_DIM)],
        out_specs=_split_rows(width),
        out_shape=_split_shapes(width),
        compiler_params=_params("arbitrary"),
    )(h, g, w, cos, sin)


def _softmax_parts(scores):
    m = functools.reduce(jnp.maximum, [jnp.max(s, axis=-1, keepdims=True) for s in scores])
    probs = [jnp.exp(s - m) for s in scores]
    z = functools.reduce(jnp.add, [jnp.sum(p, axis=-1, keepdims=True) for p in probs])
    return probs, z, m + jnp.log(z)


def _attn_prompt_kernel(q0_ref, q1_ref, q2_ref, k_ref, v_ref, o_ref, og_ref, lse_ref):
    iq = lax.broadcasted_iota(jnp.int32, (ATTN_BLOCK, 2 * ATTN_BLOCK), 0)
    ik = lax.broadcasted_iota(jnp.int32, (ATTN_BLOCK, 2 * ATTN_BLOCK), 1)
    diff = ATTN_BLOCK + iq - ik
    two_blocks = (diff >= 0) & (diff <= ATTN_BLOCK)
    own_block = (lax.broadcasted_iota(jnp.int32, (ATTN_BLOCK, ATTN_BLOCK), 0)
                 >= lax.broadcasted_iota(jnp.int32, (ATTN_BLOCK, ATTN_BLOCK), 1))
    for g, (q_ref, (_, dil)) in enumerate(zip((q0_ref, q1_ref, q2_ref), DILATED_GROUPS)):
        for res in range(dil):
            for blk in range(SEQ // dil // ATTN_BLOCK):
                first = blk * ATTN_BLOCK * dil + res
                rows = pl.ds(first, ATTN_BLOCK, stride=dil)
                if blk == 0:
                    keys, valid = rows, own_block
                else:
                    keys, valid = pl.ds(first - ATTN_BLOCK * dil, 2 * ATTN_BLOCK, stride=dil), two_blocks
                s = lax.dot_general(q_ref[rows, :].astype(BF16), k_ref[keys, :].astype(BF16), NT_DIMS,
                                    preferred_element_type=F32) * ATTN_SCALE
                (p,), z, lse = _softmax_parts([jnp.where(valid, s, -jnp.inf)])
                og_ref[g, rows, :] = jnp.dot(p.astype(BF16), v_ref[keys, :].astype(BF16),
                                             preferred_element_type=F32) / z
                lse_ref[g, rows, :] = jnp.broadcast_to(lse, (ATTN_BLOCK, HEAD_DIM))
    for row0 in range(0, SEQ, 2 * ATTN_BLOCK):
        rs = slice(row0, row0 + 2 * ATTN_BLOCK)
        lses = [lse_ref[g, rs, :] for g in range(N_DGROUPS)]
        top = functools.reduce(jnp.maximum, lses)
        wts = [jnp.exp(l - top) for l in lses]
        merged = functools.reduce(jnp.add, [w * og_ref[g, rs, :] for g, w in enumerate(wts)])
        o_ref[rs, :] = merged / functools.reduce(jnp.add, wts)


def _attn_prompt(q, kv):
    blk = (SEQ, HEAD_DIM)
    group_scratch = pltpu.VMEM((N_DGROUPS, SEQ, HEAD_DIM), F32)
    return pl.pallas_call(
        _attn_prompt_kernel,
        grid=(BATCH, N_KV_HEADS),
        in_specs=[pl.BlockSpec(blk, lambda b, h, g=g: (b, g * N_KV_HEADS + h)) for g in range(N_DGROUPS)]
        + [pl.BlockSpec(blk, lambda b, h: (b, h)), pl.BlockSpec(blk, lambda b, h: (b, N_KV_HEADS + h))],
        out_specs=pl.BlockSpec(blk, lambda b, h: (b, h)),
        out_shape=jax.ShapeDtypeStruct((N_PROMPT, KV_WIDTH), F32),
        scratch_shapes=[group_scratch, group_scratch],
        compiler_params=_params("parallel", "parallel"),
    )(q, q, q, kv, kv)


CACHE_PAGE = 16
CACHE_PAGES = PAST_LEN // CACHE_PAGE
G1_PAGES = DILATED_GROUPS[1][0] // CACHE_PAGE
G0_PAGES = DILATED_GROUPS[0][0] // CACHE_PAGE


G2_OLD_PAGES = CACHE_PAGES - G1_PAGES
QH = DEC_SEQ * N_KV_HEADS


def _sample_masks():
    row = jnp.arange(QH)[:, None]
    j, h = row // N_KV_HEADS, row % N_KV_HEADS

    def mask(n_pos, ok):
        col = jnp.arange(n_pos * N_KV_HEADS)[None, :]
        pos, hk = col // N_KV_HEADS, col % N_KV_HEADS
        return jnp.where((hk == h) & ok(pos), 0.0, -jnp.inf).astype(F32)

    n0, n1 = G0_PAGES * CACHE_PAGE, G1_PAGES * CACHE_PAGE
    past = [mask(n0, lambda c: c >= j),
            mask(n1, lambda c: (c >= j) & ((n1 + j - c) % DILATED_GROUPS[1][1] == 0)),
            mask(CACHE_PAGES * DEC_SEQ, lambda p: p % DEC_SEQ == j)]
    new = [mask(DEC_SEQ, lambda c: c <= j),
           mask(DEC_SEQ, lambda c: (c <= j) & ((j - c) % DILATED_GROUPS[1][1] == 0)),
           mask(DEC_SEQ, lambda c: c == j)]
    return past, jnp.stack(new)


def _attn_sample_kernel(q_ref, kvn_ref, ca_ref, cb_ref, m0_ref, m1_ref, m2_ref, mnew_ref, o_ref):
    def keys_values(ref, *idx):
        return tuple(ref[idx + (kv,)].reshape(-1, HEAD_DIM).astype(BF16) for kv in range(2))

    n_old = G2_OLD_PAGES * DEC_SEQ * N_KV_HEADS
    new = keys_values(kvn_ref, 0, slice(None))
    parts = [
        [(keys_values(cb_ref, 0, slice(G1_PAGES - G0_PAGES, None), slice(None)), m0_ref[...])],
        [(keys_values(cb_ref, 0, slice(None), slice(None)), m1_ref[...])],
        [(keys_values(ca_ref, 0, slice(None), slice(None)), m2_ref[:, :n_old]),
         (keys_values(cb_ref, 0, slice(None), slice(0, DEC_SEQ)), m2_ref[:, n_old:])],
    ]
    outs, lses = [], []
    for g in range(N_DGROUPS):
        q = q_ref[0, g].reshape(QH, HEAD_DIM).astype(BF16)
        group = parts[g] + [(new, mnew_ref[g])]
        scores = [lax.dot_general(q, k, NT_DIMS, preferred_element_type=F32) * ATTN_SCALE + m
                  for (k, _), m in group]
        probs, z, lse = _softmax_parts(scores)
        o = functools.reduce(jnp.add, [jnp.dot(p.astype(BF16), v, preferred_element_type=F32)
                                       for p, ((_, v), _) in zip(probs, group)])
        outs.append(o / z)
        lses.append(lse)
    top = functools.reduce(jnp.maximum, lses)
    wts = [jnp.exp(l - top) for l in lses]
    merged = functools.reduce(jnp.add, [w * o for w, o in zip(wts, outs)])
    o_ref[0] = (merged / functools.reduce(jnp.add, wts)).reshape(DEC_SEQ, N_KV_HEADS, HEAD_DIM)


def _attn_sample(q_sample, kv_sample, cache_kv):
    q5 = q_sample.reshape(DEC_BATCH, DEC_SEQ, N_DGROUPS, N_KV_HEADS, HEAD_DIM).transpose(0, 2, 1, 3, 4)
    cache = cache_kv.reshape(DEC_BATCH, CACHE_PAGES, CACHE_PAGE, 2, N_KV_HEADS, HEAD_DIM)
    (m0, m1, m2), m_new = _sample_masks()
    tile_dims = (2, N_KV_HEADS, HEAD_DIM)
    o = pl.pallas_call(
        _attn_sample_kernel,
        grid=(DEC_BATCH,),
        in_specs=[pl.BlockSpec((1, N_DGROUPS, DEC_SEQ, N_KV_HEADS, HEAD_DIM), lambda b: (b, 0, 0, 0, 0)),
                  pl.BlockSpec((1, DEC_SEQ) + tile_dims, lambda b: (b, 0, 0, 0, 0)),
                  pl.BlockSpec((1, G2_OLD_PAGES, DEC_SEQ) + tile_dims, lambda b: (b, 0, 0, 0, 0, 0)),
                  pl.BlockSpec((1, G1_PAGES, CACHE_PAGE) + tile_dims,
                               lambda b: (b, CACHE_PAGES // G1_PAGES - 1, 0, 0, 0, 0)),
                  _resident(m0.shape), _resident(m1.shape), _resident(m2.shape), _resident(m_new.shape)],
        out_specs=pl.BlockSpec((1, DEC_SEQ, N_KV_HEADS, HEAD_DIM), lambda b: (b, 0, 0, 0)),
        out_shape=jax.ShapeDtypeStruct((DEC_BATCH, DEC_SEQ, N_KV_HEADS, HEAD_DIM), F32),
        compiler_params=_params("parallel"),
    )(q5, kv_sample, cache, cache, m0, m1, m2, m_new)
    return o.reshape(N_SAMPLE, KV_WIDTH)


def _attn_out_kernel(op_ref, os_ref, h_ref, wo_ref, out_ref):
    o = _read_split(op_ref, os_ref).astype(BF16)
    out_ref[...] = h_ref[...] + jnp.dot(o, wo_ref[...], preferred_element_type=F32)


def _attn_out(o_prompt, o_sample, h, w_o):
    return pl.pallas_call(
        _attn_out_kernel,
        grid=(N_TOK // ROW_TILE,),
        in_specs=_split_rows(KV_WIDTH) + [_rows(D_MODEL), _resident((KV_WIDTH, D_MODEL))],
        out_specs=_rows(D_MODEL),
        out_shape=jax.ShapeDtypeStruct((N_TOK, D_MODEL), F32),
        compiler_params=_params("parallel"),
    )(o_prompt, o_sample, h, w_o)


def _rope_tables():
    half = HEAD_DIM // 2
    inv_freq = ROPE_THETA ** (-jnp.arange(half, dtype=F32) / half)
    pos = jnp.concatenate([jnp.tile(jnp.arange(SEQ, dtype=jnp.int32), BATCH),
                           jnp.tile(PAST_LEN + jnp.arange(DEC_SEQ, dtype=jnp.int32), DEC_BATCH)])
    ang = pos.astype(F32)[:, None] * inv_freq[None, :]
    cos, sin = jnp.cos(ang), jnp.sin(ang)
    return jnp.concatenate([cos, cos], axis=-1), jnp.concatenate([-sin, sin], axis=-1)


def _mix_weights(w_s, b_s):
    tril = jnp.tril(jnp.ones((CHUNK, CHUNK), dtype=bool))
    w_prompt = jnp.where(tril, w_s, 0.0)
    reps = CHUNK // DEC_SEQ
    small = w_prompt[:, :DEC_SEQ, :DEC_SEQ]
    eye = jnp.eye(reps, dtype=F32)
    w_sample = jnp.einsum('ab,gts->gatbs', eye, small).reshape(A_GROUPS, CHUNK, CHUNK)
    wmix = jnp.stack([w_prompt, w_sample]).astype(BF16)
    b_prompt = b_s.T
    b_sample = jnp.tile(b_s[:, :DEC_SEQ].T, (reps, 1))
    bias = jnp.repeat(jnp.stack([b_prompt, b_sample]), A_GROUP_DIM, axis=-1)
    return wmix, bias


def kernel(x_prompt, x_sample, p_prompt, p_sample, cache_kv, g_mix, a_w_in, a_ln_g, a_ln_b, a_w_s, a_b_s,
           a_w_out, g_kv, w_kv, b_w_q, b_w_o, g_ffn, peer_w_q, peer_keys, peer_u, peer_v, g_ple,
           ple_w_gate, ple_w_proj, g_final):
    xp, xs = x_prompt.reshape(N_PROMPT, D_MODEL), x_sample.reshape(N_SAMPLE, D_MODEL)
    p = jnp.concatenate([p_prompt.reshape(2, N_PROMPT, PLE_DIM), p_sample.reshape(2, N_SAMPLE, PLE_DIM)], axis=1)
    cos, sin = _rope_tables()
    u_bf, vt_bf = _peer_tables(peer_u, peer_v)
    row = lambda a: a.reshape(1, -1)

    u, v = _gmlp_in(xp, xs, row(g_mix[0]), a_w_in[0].astype(BF16), row(a_ln_g[0]), row(a_ln_b[0]))
    wmix, bias = _mix_weights(a_w_s[0], a_b_s[0])
    h = _gmlp_out(u, v, xp, xs, wmix, bias, a_w_out[0].astype(BF16))
    y = _peer(h, row(g_ffn[0]), peer_w_q[0], peer_keys[0], u_bf, vt_bf, 0)
    (h,) = _ple(h, y, p[0], row(g_ple[0]), ple_w_gate[0].astype(BF16), ple_w_proj[0].astype(BF16))

    kv_prompt, kv_sample = _proj_rope(h, row(g_kv), w_kv.astype(BF16), cos, sin, N_KV_HEADS)
    q_prompt, q_sample = _proj_rope(h, row(g_mix[1]), b_w_q[0].astype(BF16), cos, sin, N_DGROUPS * N_KV_HEADS)

    kv_sample = kv_sample.reshape(DEC_BATCH, DEC_SEQ, 2, N_KV_HEADS, HEAD_DIM)
    o_sample = _attn_sample(q_sample.reshape(DEC_BATCH, DEC_SEQ, Q_WIDTH), kv_sample, cache_kv)
    h = _attn_out(_attn_prompt(q_prompt, kv_prompt), o_sample, h, b_w_o[0].astype(BF16))
    y = _peer(h, row(g_ffn[1]), peer_w_q[1], peer_keys[1], u_bf, vt_bf, 1)
    y_prompt, y_sample = _ple(h, y, p[1], row(g_ple[1]), ple_w_gate[1].astype(BF16),
                              ple_w_proj[1].astype(BF16), row(g_final))

    y_prompt = y_prompt.reshape(BATCH, SEQ, D_MODEL)
    y_sample = y_sample.reshape(DEC_BATCH, DEC_SEQ, D_MODEL)
    a_v_prompt = jnp.stack([v[(b + 1) * SEQ - CHUNK:(b + 1) * SEQ] for b in range(BATCH)])[None]
    a_v_sample = v[N_PROMPT:].reshape(1, DEC_BATCH, DEC_SEQ, A_WIDTH)
    kv_prompt = kv_prompt.reshape(BATCH, SEQ, 2, N_KV_HEADS, HEAD_DIM)
    return (y_prompt, y_sample, a_v_prompt, a_v_sample, kv_prompt, kv_sample)
```

```python
import functools

import jax
import jax.numpy as jnp
from jax import lax
from jax.experimental import pallas as pl
from jax.experimental.pallas import tpu as pltpu

F32 = jnp.float32
BF16 = jnp.bfloat16

D_MODEL = 2048
BATCH = 4
SEQ = 2048
DEC_BATCH = 128
DEC_SEQ = 8
PAST_LEN = 2048
N_PROMPT = BATCH * SEQ
N_SAMPLE = DEC_BATCH * DEC_SEQ
N_TOK = N_PROMPT + N_SAMPLE

CHUNK = 128
A_WIDTH = D_MODEL
A_GROUPS = 8
A_GROUP_DIM = A_WIDTH // A_GROUPS

HEAD_DIM = 128
N_KV_HEADS = 8
KV_WIDTH = N_KV_HEADS * HEAD_DIM
DILATED_GROUPS = ((128, 1), (512, 4), (2048, 16))
N_DGROUPS = len(DILATED_GROUPS)
Q_WIDTH = N_DGROUPS * KV_WIDTH
ATTN_BLOCK = 128
ROPE_THETA = 10000.0
ATTN_SCALE = HEAD_DIM ** -0.5

N_KEYS = 128
N_EXPERTS = N_KEYS * N_KEYS
PEER_HEADS = 8
PEER_HALF = 128
PEER_TOPK = 16
PLE_DIM = 256
RMS_EPS = 1e-6
LN_EPS = 1e-5

V7X_VMEM_BYTES = 64 * 1024 * 1024
VMEM_LIMIT = V7X_VMEM_BYTES - 8 * 1024 * 1024
LANES = 128

ROW_TILE = 256
PEER_TOK_TILE = 512
PEER_EXP_TILE = 1024
PEER_I1_PER_STEP = PEER_EXP_TILE // N_KEYS

NT_DIMS = (((1,), (1,)), ((), ()))


def _params(*sem):
    return pltpu.CompilerParams(dimension_semantics=sem, vmem_limit_bytes=VMEM_LIMIT)


def _resident(shape):
    return pl.BlockSpec(shape, lambda *_: (0,) * len(shape), pipeline_mode=pl.Buffered(1))


def _rows(width, tile=ROW_TILE):
    return pl.BlockSpec((tile, width), lambda i: (i, 0))


PROMPT_TILES = N_PROMPT // ROW_TILE


def _split_rows(width):
    return [pl.BlockSpec((ROW_TILE, width), lambda i: (jnp.minimum(i, PROMPT_TILES - 1), 0)),
            pl.BlockSpec((ROW_TILE, width), lambda i: (jnp.maximum(i - PROMPT_TILES, 0), 0))]


def _split_shapes(width):
    return [jax.ShapeDtypeStruct((N_PROMPT, width), F32), jax.ShapeDtypeStruct((N_SAMPLE, width), F32)]


def _read_split(prompt_ref, sample_ref):
    return jnp.where(pl.program_id(0) < PROMPT_TILES, prompt_ref[...], sample_ref[...])


def _for_split(prompt_ref, sample_ref, body):
    step = pl.program_id(0)

    @pl.when(step < PROMPT_TILES)
    def _():
        body(prompt_ref)

    @pl.when(step >= PROMPT_TILES)
    def _():
        body(sample_ref)


def _rms(x, g):
    return x * lax.rsqrt(jnp.mean(x * x, axis=-1, keepdims=True) + RMS_EPS) * g


def _gmlp_in_kernel(xp_ref, xs_ref, g_ref, w_ref, lng_ref, lnb_ref, u_ref, v_ref):
    xn = _rms(_read_split(xp_ref, xs_ref), g_ref[...]).astype(BF16)
    u_ref[...] = jax.nn.gelu(jnp.dot(xn, w_ref[:, :A_WIDTH], preferred_element_type=F32))
    v = jax.nn.gelu(jnp.dot(xn, w_ref[:, A_WIDTH:], preferred_element_type=F32))
    mu = jnp.mean(v, axis=-1, keepdims=True)
    vc = v - mu
    var = jnp.mean(vc * vc, axis=-1, keepdims=True)
    v_ref[...] = vc * lax.rsqrt(var + LN_EPS) * lng_ref[...] + lnb_ref[...]


def _gmlp_in(x_prompt, x_sample, g, w_in, ln_g, ln_b):
    return pl.pallas_call(
        _gmlp_in_kernel,
        grid=(N_TOK // ROW_TILE,),
        in_specs=_split_rows(D_MODEL) + [_resident((1, D_MODEL)), _resident((D_MODEL, 2 * A_WIDTH)),
                                         _resident((1, A_WIDTH)), _resident((1, A_WIDTH))],
        out_specs=[_rows(A_WIDTH), _rows(A_WIDTH)],
        out_shape=[jax.ShapeDtypeStruct((N_TOK, A_WIDTH), F32)] * 2,
        compiler_params=_params("parallel"),
    )(x_prompt, x_sample, g, w_in, ln_g, ln_b)


def _gmlp_out_kernel(u_ref, v_ref, xp_ref, xs_ref, wmix_ref, bias_ref, wout_ref, o_ref, um_ref):
    for c in range(ROW_TILE // CHUNK):
        rows = slice(c * CHUNK, (c + 1) * CHUNK)
        for g in range(A_GROUPS):
            cols = slice(g * A_GROUP_DIM, (g + 1) * A_GROUP_DIM)
            mixed = jnp.dot(wmix_ref[0, g], v_ref[rows, cols].astype(BF16),
                            preferred_element_type=F32) + bias_ref[0, :, cols]
            um_ref[rows, cols] = (u_ref[rows, cols] * mixed).astype(BF16)
    o_ref[...] = _read_split(xp_ref, xs_ref) + jnp.dot(um_ref[...], wout_ref[...], preferred_element_type=F32)


def _gmlp_out(u, v, x_prompt, x_sample, wmix, bias, w_out):
    return pl.pallas_call(
        _gmlp_out_kernel,
        grid=(N_TOK // ROW_TILE,),
        in_specs=[_rows(A_WIDTH), _rows(A_WIDTH)] + _split_rows(D_MODEL)
        + [pl.BlockSpec((1, A_GROUPS, CHUNK, CHUNK), lambda i: (i // PROMPT_TILES, 0, 0, 0)),
           pl.BlockSpec((1, CHUNK, A_WIDTH), lambda i: (i // PROMPT_TILES, 0, 0)),
           _resident((A_WIDTH, D_MODEL))],
        out_specs=_rows(D_MODEL),
        out_shape=jax.ShapeDtypeStruct((N_TOK, D_MODEL), F32),
        scratch_shapes=[pltpu.VMEM((ROW_TILE, A_WIDTH), BF16)],
        compiler_params=_params("parallel"),
    )(u, v, x_prompt, x_sample, wmix, bias, w_out)


PEER_NTOP = PEER_TOPK + 1
TOP_SECOND = 24
TOP_ROWS = 48


SUBLANES = 8
SORT16 = ((0, 1), (2, 3), (0, 2), (1, 3), (1, 2), (4, 5), (6, 7), (4, 6), (5, 7), (5, 6), (0, 4), (2, 6), (2, 4),
          (1, 5), (3, 7), (3, 5), (1, 2), (3, 4), (5, 6), (8, 9), (10, 11), (8, 10), (9, 11), (9, 10), (12, 13),
          (14, 15), (12, 14), (13, 15), (13, 14), (8, 12), (10, 14), (10, 12), (9, 13), (11, 15), (11, 13),
          (9, 10), (11, 12), (13, 14), (0, 8), (4, 12), (4, 8), (2, 10), (6, 14), (6, 10), (2, 4), (6, 8),
          (10, 12), (1, 9), (5, 13), (5, 9), (3, 11), (7, 15), (7, 11), (3, 5), (7, 9), (11, 13), (1, 2), (3, 4),
          (5, 6), (7, 8), (9, 10), (11, 12), (13, 14))


def _top_values(s, top_ref, base):
    lists = [s[SUBLANES * j:SUBLANES * (j + 1), :] for j in range(N_KEYS // SUBLANES)]
    for i, j in SORT16:
        lists[i], lists[j] = jnp.maximum(lists[i], lists[j]), jnp.minimum(lists[i], lists[j])
    for p in range(PEER_NTOP):
        m = jnp.max(lists[0], axis=0, keepdims=True)
        top_ref[base + p:base + p + 1, :] = m
        popped = lists[0] == m
        for d in range(min(len(lists), PEER_NTOP - 1 - p)):
            below = lists[d + 1] if d + 1 < len(lists) else -jnp.inf
            lists[d] = jnp.where(popped, below, lists[d])


def _rank_head(s1, s2, top_ref):
    _top_values(s1, top_ref, 0)
    _top_values(s2, top_ref, TOP_SECOND)

    def first(lo, hi):
        return top_ref[lo:hi, :]

    def second(lo, hi):
        return top_ref[TOP_SECOND + lo:TOP_SECOND + hi, :]

    m1, m2 = first(0, 1), second(0, 1)
    cands = [first(0, 8) + second(b, b + 1) for b in range(3)]
    cands += [first(8, 16) + m2, first(16, 24) + m2]
    cands += [m1 + second(0, 8), m1 + second(8, 16), m1 + second(16, 24)]
    cands += [first(a, a + 1) + second(0, 8) for a in range(1, 4)]
    top = m1 + m2
    zsum = jnp.zeros_like(top)
    ranked = []
    for k in range(PEER_NTOP):
        m = jnp.max(functools.reduce(jnp.maximum, cands), axis=0, keepdims=True)
        ranked.append(m)
        if k < PEER_TOPK:
            zsum = zsum + jnp.exp(m - top)
        if k + 1 < PEER_NTOP:
            cands = [jnp.where(c == m, -jnp.inf, c) for c in cands]
    thr = 0.5 * (ranked[PEER_TOPK - 1] + ranked[PEER_TOPK])
    return jnp.exp((thr - m2) - s1), jnp.exp(s1 - m1) / zsum, jnp.exp(s2 - m2)


def _peer_route_kernel(h_ref, g_ref, wpq_ref, keys_ref, xnt_ref, c1_ref, g1_ref, g2_ref, top_ref):
    xn_f32 = _rms(h_ref[...], g_ref[...])
    xn = xn_f32.astype(BF16)
    xnt_ref[...] = xn_f32.T.astype(BF16)
    q = jnp.dot(xn, wpq_ref[...], preferred_element_type=F32).astype(BF16)
    top_ref[...] = jnp.full(top_ref.shape, -jnp.inf, F32)
    for hd in range(PEER_HEADS):
        c0 = hd * 2 * PEER_HALF
        s1 = lax.dot_general(keys_ref[hd, 0], q[:, c0:c0 + PEER_HALF], NT_DIMS,
                             preferred_element_type=F32)
        s2 = lax.dot_general(keys_ref[hd, 1], q[:, c0 + PEER_HALF:c0 + 2 * PEER_HALF], NT_DIMS,
                             preferred_element_type=F32)
        c1_ref[hd], g1_ref[hd], g2_ref[hd] = _rank_head(s1, s2, top_ref)


def _peer_route(h, g, w_pq, keys):
    route_spec = pl.BlockSpec((PEER_HEADS, N_KEYS, ROW_TILE), lambda i: (0, 0, i))
    route_shape = jax.ShapeDtypeStruct((PEER_HEADS, N_KEYS, N_TOK), F32)
    return pl.pallas_call(
        _peer_route_kernel,
        grid=(N_TOK // ROW_TILE,),
        in_specs=[_rows(D_MODEL), _resident((1, D_MODEL)), _resident((D_MODEL, D_MODEL)),
                  _resident((PEER_HEADS, 2, N_KEYS, PEER_HALF))],
        out_specs=[pl.BlockSpec((D_MODEL, ROW_TILE), lambda i: (0, i))] + [route_spec] * 3,
        out_shape=[jax.ShapeDtypeStruct((D_MODEL, N_TOK), BF16)] + [route_shape] * 3,
        scratch_shapes=[pltpu.VMEM((TOP_ROWS, ROW_TILE), F32)],
        compiler_params=_params("parallel"),
    )(h, g, w_pq, keys)


PEER_TOK_STEPS = N_TOK // PEER_TOK_TILE
PEER_EXP_STEPS = N_EXPERTS // PEER_EXP_TILE
PEER_STEPS = PEER_TOK_STEPS * PEER_EXP_STEPS
PEER_LAG = 2
PEER_SUB_KEYS = 16
PEER_CHUNKS = 2


def _peer_dense_kernel(xt_ref, u_ref, vt_ref, c1_ref, g1_ref, g2_ref, y_ref,
                       acc_ref, act_a_ref, act_b_ref, w_a_ref, w_b_ref):
    s = pl.program_id(0)
    contract_exp_step = jnp.maximum(s - PEER_LAG, 0) % PEER_EXP_STEPS

    @pl.when(s == 0)
    def _():
        act_b_ref[...] = jnp.zeros_like(act_b_ref)
        w_a_ref[...] = jnp.zeros_like(w_a_ref)

    @pl.when(contract_exp_step == 0)
    def _():
        acc_ref[...] = jnp.zeros_like(acc_ref)

    def step(act_new_ref, act_old_ref, w_new_ref, w_old_ref):
        act_rows = PEER_EXP_TILE // PEER_CHUNKS
        acc_rows = D_MODEL // PEER_CHUNKS
        first_keys = PEER_I1_PER_STEP // PEER_CHUNKS

        def first_key_row(ref, hd, k, ls, i):
            rows = [ref[hd, c * first_keys + k:c * first_keys + k + 1, ls] for c in range(PEER_CHUNKS)]
            row = rows[-1]
            for c in range(PEER_CHUNKS - 2, -1, -1):
                row = jnp.where(i == c, rows[c], row)
            return row

        def chunk(i, carry):
            a0 = pl.multiple_of(i * act_rows, act_rows)
            c0 = pl.multiple_of(i * acc_rows, acc_rows)
            act_new_ref[pl.ds(a0, act_rows), :] = jnp.dot(
                u_ref[0, pl.ds(a0, act_rows), :], xt_ref[...], preferred_element_type=F32)
            acc_ref[pl.ds(c0, acc_rows), :] += jnp.dot(vt_ref[0, 0, pl.ds(c0, acc_rows), :], w_old_ref[...],
                                                       preferred_element_type=F32)
            for k in range(first_keys):
                for lt in range(PEER_TOK_TILE // LANES):
                    ls = slice(lt * LANES, (lt + 1) * LANES)
                    c1 = [first_key_row(c1_ref, hd, k, ls, i) for hd in range(PEER_HEADS)]
                    g1 = [first_key_row(g1_ref, hd, k, ls, i) for hd in range(PEER_HEADS)]
                    for k0 in range(0, N_KEYS, PEER_SUB_KEYS):
                        ks = slice(k0, k0 + PEER_SUB_KEYS)
                        rs = pl.ds(pl.multiple_of((i * first_keys + k) * N_KEYS + k0, PEER_SUB_KEYS),
                                   PEER_SUB_KEYS)
                        coef = None
                        for hd in range(PEER_HEADS):
                            g2 = g2_ref[hd, ks, ls]
                            term = jnp.where(g2 >= c1[hd], g2, 0.0) * g1[hd]
                            coef = term if coef is None else coef + term
                        w_new_ref[rs, ls] = (jax.nn.gelu(act_old_ref[rs, ls]) * coef).astype(BF16)
            return carry

        lax.fori_loop(0, PEER_CHUNKS, chunk, 0)

    @pl.when(s % 2 == 0)
    def _():
        step(act_a_ref, act_b_ref, w_b_ref, w_a_ref)

    @pl.when(s % 2 == 1)
    def _():
        step(act_b_ref, act_a_ref, w_a_ref, w_b_ref)

    @pl.when((contract_exp_step == PEER_EXP_STEPS - 1) & (s >= PEER_LAG))
    def _():
        y_ref[...] = acc_ref[...].T


def _peer_dense(xnt, u_bf, vt_bf, layer, c1, g1, g2):
    def tile(s, lag):
        return jnp.clip(s - lag, 0, PEER_STEPS - 1)

    def tok(s, lag):
        return tile(s, lag) // PEER_EXP_STEPS

    def exp(s, lag):
        return tile(s, lag) % PEER_EXP_STEPS

    first_spec = pl.BlockSpec((PEER_HEADS, PEER_I1_PER_STEP, PEER_TOK_TILE), lambda s: (0, exp(s, 1), tok(s, 1)))
    act_shape = pltpu.VMEM((PEER_EXP_TILE, PEER_TOK_TILE), F32)
    w_shape = pltpu.VMEM((PEER_EXP_TILE, PEER_TOK_TILE), BF16)
    return pl.pallas_call(
        _peer_dense_kernel,
        grid=(PEER_STEPS + PEER_LAG,),
        in_specs=[pl.BlockSpec((D_MODEL, PEER_TOK_TILE), lambda s: (0, tok(s, 0))),
                  pl.BlockSpec((1, PEER_EXP_TILE, D_MODEL), lambda s: (layer, exp(s, 0), 0)),
                  pl.BlockSpec((1, 1, D_MODEL, PEER_EXP_TILE), lambda s: (layer, exp(s, PEER_LAG), 0, 0)),
                  first_spec, first_spec,
                  pl.BlockSpec((PEER_HEADS, N_KEYS, PEER_TOK_TILE), lambda s: (0, 0, tok(s, 1)))],
        out_specs=pl.BlockSpec((PEER_TOK_TILE, D_MODEL), lambda s: (tok(s, PEER_LAG), 0)),
        out_shape=jax.ShapeDtypeStruct((N_TOK, D_MODEL), F32),
        scratch_shapes=[pltpu.VMEM((D_MODEL, PEER_TOK_TILE), F32), act_shape, act_shape, w_shape, w_shape],
        compiler_params=_params("arbitrary"),
    )(xnt, u_bf, vt_bf, c1, g1, g2)


TABLE_TILE = 512


def _peer_tables_kernel(u_ref, v_ref, ub_ref, vt_ref):
    ub_ref[0] = u_ref[0].astype(BF16)
    vt_ref[0, 0] = v_ref[0].T.astype(BF16)


def _peer_tables(peer_u, peer_v):
    n_layers = peer_u.shape[0]
    per_tile = PEER_EXP_TILE // TABLE_TILE
    rows = pl.BlockSpec((1, TABLE_TILE, D_MODEL), lambda l, j: (l, j, 0))
    return pl.pallas_call(
        _peer_tables_kernel,
        grid=(n_layers, N_EXPERTS // TABLE_TILE),
        in_specs=[rows, rows],
        out_specs=[rows, pl.BlockSpec((1, 1, D_MODEL, TABLE_TILE), lambda l, j: (l, j // per_tile, 0, j % per_tile))],
        out_shape=[jax.ShapeDtypeStruct((n_layers, N_EXPERTS, D_MODEL), BF16),
                   jax.ShapeDtypeStruct((n_layers, PEER_EXP_STEPS, D_MODEL, PEER_EXP_TILE), BF16)],
        compiler_params=_params("parallel", "parallel"),
    )(peer_u, peer_v)


def _peer(h, g, w_pq, keys, u_bf, vt_bf, layer):
    xnt, c1, g1, g2 = _peer_route(h, g, w_pq.astype(BF16), keys.astype(BF16))
    return _peer_dense(xnt, u_bf, vt_bf, layer, c1, g1, g2)


def _ple_kernel(h_ref, y_ref, p_ref, g_ref, wg_ref, wp_ref, *rest, final):
    h = h_ref[...] + y_ref[...]
    gate = jax.nn.sigmoid(jnp.dot(_rms(h, g_ref[...]).astype(BF16), wg_ref[...], preferred_element_type=F32))
    out = h + gate * jnp.dot(p_ref[...].astype(BF16), wp_ref[...], preferred_element_type=F32)
    if final:
        gf_ref, yp_ref, ys_ref = rest
        y_final = _rms(out, gf_ref[...])

        def store(ref):
            ref[...] = y_final

        _for_split(yp_ref, ys_ref, store)
    else:
        (o_ref,) = rest
        o_ref[...] = out


def _ple(h, y, p, g, w_gate, w_proj, g_final=None):
    final = g_final is not None
    in_specs = [_rows(D_MODEL), _rows(D_MODEL), _rows(PLE_DIM), _resident((1, D_MODEL)),
                _resident((D_MODEL, D_MODEL)), _resident((PLE_DIM, D_MODEL))]
    args = [h, y, p, g, w_gate, w_proj]
    out_specs, out_shape = [_rows(D_MODEL)], [jax.ShapeDtypeStruct((N_TOK, D_MODEL), F32)]
    if final:
        in_specs.append(_resident((1, D_MODEL)))
        args.append(g_final)
        out_specs, out_shape = _split_rows(D_MODEL), _split_shapes(D_MODEL)
    return pl.pallas_call(
        functools.partial(_ple_kernel, final=final),
        grid=(N_TOK // ROW_TILE,),
        in_specs=in_specs,
        out_specs=out_specs,
        out_shape=out_shape,
        compiler_params=_params("arbitrary" if final else "parallel"),
    )(*args)


def _proj_rope_kernel(h_ref, g_ref, w_ref, cos_ref, sin_ref, op_ref, os_ref, *, n_rope_heads, width):
    def body(o_ref):
        xn = _rms(h_ref[...], g_ref[...]).astype(BF16)
        cos = cos_ref[...]
        sin = sin_ref[...]
        for c0 in range(0, width, KV_WIDTH):
            z = jnp.dot(xn, w_ref[:, c0:c0 + KV_WIDTH], preferred_element_type=F32)
            for hd in range(N_KV_HEADS):
                cs = slice(hd * HEAD_DIM, (hd + 1) * HEAD_DIM)
                zh = z[:, cs]
                if c0 // HEAD_DIM + hd < n_rope_heads:
                    zh = zh * cos + pltpu.roll(zh, HEAD_DIM // 2, 1) * sin
                o_ref[:, c0 + hd * HEAD_DIM:c0 + (hd + 1) * HEAD_DIM] = zh

    _for_split(op_ref, os_ref, body)


def _proj_rope(h, g, w, cos, sin, n_rope_heads):
    width = w.shape[1]
    return pl.pallas_call(
        functools.partial(_proj_rope_kernel, n_rope_heads=n_rope_heads, width=width),
        grid=(N_TOK // ROW_TILE,),
        in_specs=[_rows(D_MODEL), _resident((1, D_MODEL)), _resident((D_MODEL, width)),
                  _rows(HEAD_DIM), _rows(HEAD_DIM)],
        out_specs=_split_rows(width),
        out_shape=_split_shapes(width),
        compiler_params=_params("arbitrary"),
    )(h, g, w, cos, sin)


def _softmax_parts(scores):
    m = functools.reduce(jnp.maximum, [jnp.max(s, axis=-1, keepdims=True) for s in scores])
    probs = [jnp.exp(s - m) for s in scores]
    z = functools.reduce(jnp.add, [jnp.sum(p, axis=-1, keepdims=True) for p in probs])
    return probs, z, m + jnp.log(z)


def _attn_prompt_kernel(q0_ref, q1_ref, q2_ref, k_ref, v_ref, o_ref, og_ref, lse_ref):
    iq = lax.broadcasted_iota(jnp.int32, (ATTN_BLOCK, 2 * ATTN_BLOCK), 0)
    ik = lax.broadcasted_iota(jnp.int32, (ATTN_BLOCK, 2 * ATTN_BLOCK), 1)
    diff = ATTN_BLOCK + iq - ik
    two_blocks = (diff >= 0) & (diff <= ATTN_BLOCK)
    own_block = (lax.broadcasted_iota(jnp.int32, (ATTN_BLOCK, ATTN_BLOCK), 0)
                 >= lax.broadcasted_iota(jnp.int32, (ATTN_BLOCK, ATTN_BLOCK), 1))
    for g, (q_ref, (_, dil)) in enumerate(zip((q0_ref, q1_ref, q2_ref), DILATED_GROUPS)):
        for res in range(dil):
            for blk in range(SEQ // dil // ATTN_BLOCK):
                first = blk * ATTN_BLOCK * dil + res
                rows = pl.ds(first, ATTN_BLOCK, stride=dil)
                if blk == 0:
                    keys, valid = rows, own_block
                else:
                    keys, valid = pl.ds(first - ATTN_BLOCK * dil, 2 * ATTN_BLOCK, stride=dil), two_blocks
                s = lax.dot_general(q_ref[rows, :].astype(BF16), k_ref[keys, :].astype(BF16), NT_DIMS,
                                    preferred_element_type=F32) * ATTN_SCALE
                (p,), z, lse = _softmax_parts([jnp.where(valid, s, -jnp.inf)])
                og_ref[g, rows, :] = jnp.dot(p.astype(BF16), v_ref[keys, :].astype(BF16),
                                             preferred_element_type=F32) / z
                lse_ref[g, rows, :] = jnp.broadcast_to(lse, (ATTN_BLOCK, HEAD_DIM))
    for row0 in range(0, SEQ, 2 * ATTN_BLOCK):
        rs = slice(row0, row0 + 2 * ATTN_BLOCK)
        lses = [lse_ref[g, rs, :] for g in range(N_DGROUPS)]
        top = functools.reduce(jnp.maximum, lses)
        wts = [jnp.exp(l - top) for l in lses]
        merged = functools.reduce(jnp.add, [w * og_ref[g, rs, :] for g, w in enumerate(wts)])
        o_ref[rs, :] = merged / functools.reduce(jnp.add, wts)


def _attn_prompt(q, kv):
    blk = (SEQ, HEAD_DIM)
    group_scratch = pltpu.VMEM((N_DGROUPS, SEQ, HEAD_DIM), F32)
    return pl.pallas_call(
        _attn_prompt_kernel,
        grid=(BATCH, N_KV_HEADS),
        in_specs=[pl.BlockSpec(blk, lambda b, h, g=g: (b, g * N_KV_HEADS + h)) for g in range(N_DGROUPS)]
        + [pl.BlockSpec(blk, lambda b, h: (b, h)), pl.BlockSpec(blk, lambda b, h: (b, N_KV_HEADS + h))],
        out_specs=pl.BlockSpec(blk, lambda b, h: (b, h)),
        out_shape=jax.ShapeDtypeStruct((N_PROMPT, KV_WIDTH), F32),
        scratch_shapes=[group_scratch, group_scratch],
        compiler_params=_params("parallel", "parallel"),
    )(q, q, q, kv, kv)


CACHE_PAGE = 16
CACHE_PAGES = PAST_LEN // CACHE_PAGE
G1_PAGES = DILATED_GROUPS[1][0] // CACHE_PAGE
G0_PAGES = DILATED_GROUPS[0][0] // CACHE_PAGE


G2_OLD_PAGES = CACHE_PAGES - G1_PAGES
QH = DEC_SEQ * N_KV_HEADS


def _sample_masks():
    row = jnp.arange(QH)[:, None]
    j, h = row // N_KV_HEADS, row % N_KV_HEADS

    def mask(n_pos, ok):
        col = jnp.arange(n_pos * N_KV_HEADS)[None, :]
        pos, hk = col // N_KV_HEADS, col % N_KV_HEADS
        return jnp.where((hk == h) & ok(pos), 0.0, -jnp.inf).astype(F32)

    n0, n1 = G0_PAGES * CACHE_PAGE, G1_PAGES * CACHE_PAGE
    past = [mask(n0, lambda c: c >= j),
            mask(n1, lambda c: (c >= j) & ((n1 + j - c) % DILATED_GROUPS[1][1] == 0)),
            mask(CACHE_PAGES * DEC_SEQ, lambda p: p % DEC_SEQ == j)]
    new = [mask(DEC_SEQ, lambda c: c <= j),
           mask(DEC_SEQ, lambda c: (c <= j) & ((j - c) % DILATED_GROUPS[1][1] == 0)),
           mask(DEC_SEQ, lambda c: c == j)]
    return past, jnp.stack(new)


def _attn_sample_kernel(q_ref, kvn_ref, ca_ref, cb_ref, m0_ref, m1_ref, m2_ref, mnew_ref, o_ref):
    def keys_values(ref, *idx):
        return tuple(ref[idx + (kv,)].reshape(-1, HEAD_DIM).astype(BF16) for kv in range(2))

    n_old = G2_OLD_PAGES * DEC_SEQ * N_KV_HEADS
    new = keys_values(kvn_ref, 0, slice(None))
    parts = [
        [(keys_values(cb_ref, 0, slice(G1_PAGES - G0_PAGES, None), slice(None)), m0_ref[...])],
        [(keys_values(cb_ref, 0, slice(None), slice(None)), m1_ref[...])],
        [(keys_values(ca_ref, 0, slice(None), slice(None)), m2_ref[:, :n_old]),
         (keys_values(cb_ref, 0, slice(None), slice(0, DEC_SEQ)), m2_ref[:, n_old:])],
    ]
    outs, lses = [], []
    for g in range(N_DGROUPS):
        q = q_ref[0, g].reshape(QH, HEAD_DIM).astype(BF16)
        group = parts[g] + [(new, mnew_ref[g])]
        scores = [lax.dot_general(q, k, NT_DIMS, preferred_element_type=F32) * ATTN_SCALE + m
                  for (k, _), m in group]
        probs, z, lse = _softmax_parts(scores)
        o = functools.reduce(jnp.add, [jnp.dot(p.astype(BF16), v, preferred_element_type=F32)
                                       for p, ((_, v), _) in zip(probs, group)])
        outs.append(o / z)
        lses.append(lse)
    top = functools.reduce(jnp.maximum, lses)
    wts = [jnp.exp(l - top) for l in lses]
    merged = functools.reduce(jnp.add, [w * o for w, o in zip(wts, outs)])
    o_ref[0] = (merged / functools.reduce(jnp.add, wts)).reshape(DEC_SEQ, N_KV_HEADS, HEAD_DIM)


def _attn_sample(q_sample, kv_sample, cache_kv):
    q5 = q_sample.reshape(DEC_BATCH, DEC_SEQ, N_DGROUPS, N_KV_HEADS, HEAD_DIM).transpose(0, 2, 1, 3, 4)
    cache = cache_kv.reshape(DEC_BATCH, CACHE_PAGES, CACHE_PAGE, 2, N_KV_HEADS, HEAD_DIM)
    (m0, m1, m2), m_new = _sample_masks()
    tile_dims = (2, N_KV_HEADS, HEAD_DIM)
    o = pl.pallas_call(
        _attn_sample_kernel,
        grid=(DEC_BATCH,),
        in_specs=[pl.BlockSpec((1, N_DGROUPS, DEC_SEQ, N_KV_HEADS, HEAD_DIM), lambda b: (b, 0, 0, 0, 0)),
                  pl.BlockSpec((1, DEC_SEQ) + tile_dims, lambda b: (b, 0, 0, 0, 0)),
                  pl.BlockSpec((1, G2_OLD_PAGES, DEC_SEQ) + tile_dims, lambda b: (b, 0, 0, 0, 0, 0)),
                  pl.BlockSpec((1, G1_PAGES, CACHE_PAGE) + tile_dims,
                               lambda b: (b, CACHE_PAGES // G1_PAGES - 1, 0, 0, 0, 0)),
                  _resident(m0.shape), _resident(m1.shape), _resident(m2.shape), _resident(m_new.shape)],
        out_specs=pl.BlockSpec((1, DEC_SEQ, N_KV_HEADS, HEAD_DIM), lambda b: (b, 0, 0, 0)),
        out_shape=jax.ShapeDtypeStruct((DEC_BATCH, DEC_SEQ, N_KV_HEADS, HEAD_DIM), F32),
        compiler_params=_params("parallel"),
    )(q5, kv_sample, cache, cache, m0, m1, m2, m_new)
    return o.reshape(N_SAMPLE, KV_WIDTH)


def _attn_out_kernel(op_ref, os_ref, h_ref, wo_ref, out_ref):
    o = _read_split(op_ref, os_ref).astype(BF16)
    out_ref[...] = h_ref[...] + jnp.dot(o, wo_ref[...], preferred_element_type=F32)


def _attn_out(o_prompt, o_sample, h, w_o):
    return pl.pallas_call(
        _attn_out_kernel,
        grid=(N_TOK // ROW_TILE,),
        in_specs=_split_rows(KV_WIDTH) + [_rows(D_MODEL), _resident((KV_WIDTH, D_MODEL))],
        out_specs=_rows(D_MODEL),
        out_shape=jax.ShapeDtypeStruct((N_TOK, D_MODEL), F32),
        compiler_params=_params("parallel"),
    )(o_prompt, o_sample, h, w_o)


def _rope_tables():
    half = HEAD_DIM // 2
    inv_freq = ROPE_THETA ** (-jnp.arange(half, dtype=F32) / half)
    pos = jnp.concatenate([jnp.tile(jnp.arange(SEQ, dtype=jnp.int32), BATCH),
                           jnp.tile(PAST_LEN + jnp.arange(DEC_SEQ, dtype=jnp.int32), DEC_BATCH)])
    ang = pos.astype(F32)[:, None] * inv_freq[None, :]
    cos, sin = jnp.cos(ang), jnp.sin(ang)
    return jnp.concatenate([cos, cos], axis=-1), jnp.concatenate([-sin, sin], axis=-1)


def _mix_weights(w_s, b_s):
    tril = jnp.tril(jnp.ones((CHUNK, CHUNK), dtype=bool))
    w_prompt = jnp.where(tril, w_s, 0.0)
    reps = CHUNK // DEC_SEQ
    small = w_prompt[:, :DEC_SEQ, :DEC_SEQ]
    eye = jnp.eye(reps, dtype=F32)
    w_sample = jnp.einsum('ab,gts->gatbs', eye, small).reshape(A_GROUPS, CHUNK, CHUNK)
    wmix = jnp.stack([w_prompt, w_sample]).astype(BF16)
    b_prompt = b_s.T
    b_sample = jnp.tile(b_s[:, :DEC_SEQ].T, (reps, 1))
    bias = jnp.repeat(jnp.stack([b_prompt, b_sample]), A_GROUP_DIM, axis=-1)
    return wmix, bias


def kernel(x_prompt, x_sample, p_prompt, p_sample, cache_kv, g_mix, a_w_in, a_ln_g, a_ln_b, a_w_s, a_b_s,
           a_w_out, g_kv, w_kv, b_w_q, b_w_o, g_ffn, peer_w_q, peer_keys, peer_u, peer_v, g_ple,
           ple_w_gate, ple_w_proj, g_final):
    xp, xs = x_prompt.reshape(N_PROMPT, D_MODEL), x_sample.reshape(N_SAMPLE, D_MODEL)
    p = jnp.concatenate([p_prompt.reshape(2, N_PROMPT, PLE_DIM), p_sample.reshape(2, N_SAMPLE, PLE_DIM)], axis=1)
    cos, sin = _rope_tables()
    u_bf, vt_bf = _peer_tables(peer_u, peer_v)
    row = lambda a: a.reshape(1, -1)

    u, v = _gmlp_in(xp, xs, row(g_mix[0]), a_w_in[0].astype(BF16), row(a_ln_g[0]), row(a_ln_b[0]))
    wmix, bias = _mix_weights(a_w_s[0], a_b_s[0])
    h = _gmlp_out(u, v, xp, xs, wmix, bias, a_w_out[0].astype(BF16))
    y = _peer(h, row(g_ffn[0]), peer_w_q[0], peer_keys[0], u_bf, vt_bf, 0)
    (h,) = _ple(h, y, p[0], row(g_ple[0]), ple_w_gate[0].astype(BF16), ple_w_proj[0].astype(BF16))

    kv_prompt, kv_sample = _proj_rope(h, row(g_kv), w_kv.astype(BF16), cos, sin, N_KV_HEADS)
    q_prompt, q_sample = _proj_rope(h, row(g_mix[1]), b_w_q[0].astype(BF16), cos, sin, N_DGROUPS * N_KV_HEADS)

    kv_sample = kv_sample.reshape(DEC_BATCH, DEC_SEQ, 2, N_KV_HEADS, HEAD_DIM)
    o_sample = _attn_sample(q_sample.reshape(DEC_BATCH, DEC_SEQ, Q_WIDTH), kv_sample, cache_kv)
    h = _attn_out(_attn_prompt(q_prompt, kv_prompt), o_sample, h, b_w_o[0].astype(BF16))
    y = _peer(h, row(g_ffn[1]), peer_w_q[1], peer_keys[1], u_bf, vt_bf, 1)
    y_prompt, y_sample = _ple(h, y, p[1], row(g_ple[1]), ple_w_gate[1].astype(BF16),
                              ple_w_proj[1].astype(BF16), row(g_final))

    y_prompt = y_prompt.reshape(BATCH, SEQ, D_MODEL)
    y_sample = y_sample.reshape(DEC_BATCH, DEC_SEQ, D_MODEL)
    a_v_prompt = jnp.stack([v[(b + 1) * SEQ - CHUNK:(b + 1) * SEQ] for b in range(BATCH)])[None]
    a_v_sample = v[N_PROMPT:].reshape(1, DEC_BATCH, DEC_SEQ, A_WIDTH)
    kv_prompt = kv_prompt.reshape(BATCH, SEQ, 2, N_KV_HEADS, HEAD_DIM)
    return (y_prompt, y_sample, a_v_prompt, a_v_sample, kv_prompt, kv_sample)
```

```python
import functools

import jax
import jax.numpy as jnp
from jax import lax
from jax.experimental import pallas as pl
from jax.experimental.pallas import tpu as pltpu

F32 = jnp.float32
BF16 = jnp.bfloat16

D_MODEL = 2048
BATCH = 4
SEQ = 2048
DEC_BATCH = 128
DEC_SEQ = 8
PAST_LEN = 2048
N_PROMPT = BATCH * SEQ
N_SAMPLE = DEC_BATCH * DEC_SEQ
N_TOK = N_PROMPT + N_SAMPLE

CHUNK = 128
A_WIDTH = D_MODEL
A_GROUPS = 8
A_GROUP_DIM = A_WIDTH // A_GROUPS

HEAD_DIM = 128
N_KV_HEADS = 8
KV_WIDTH = N_KV_HEADS * HEAD_DIM
DILATED_GROUPS = ((128, 1), (512, 4), (2048, 16))
N_DGROUPS = len(DILATED_GROUPS)
Q_WIDTH = N_DGROUPS * KV_WIDTH
ATTN_BLOCK = 128
ATTN_BATCH = 8
ROPE_THETA = 10000.0
ATTN_SCALE = HEAD_DIM ** -0.5

N_KEYS = 128
N_EXPERTS = N_KEYS * N_KEYS
PEER_HEADS = 8
PEER_HALF = 128
PEER_TOPK = 16
PLE_DIM = 256
RMS_EPS = 1e-6
LN_EPS = 1e-5

V7X_VMEM_BYTES = 64 * 1024 * 1024
VMEM_LIMIT = V7X_VMEM_BYTES - 8 * 1024 * 1024
LANES = 128

ROW_TILE = 256
PEER_TOK_TILE = 512
PEER_EXP_TILE = 1024
PEER_I1_PER_STEP = PEER_EXP_TILE // N_KEYS

NT_DIMS = (((1,), (1,)), ((), ()))


def _params(*sem):
    return pltpu.CompilerParams(dimension_semantics=sem, vmem_limit_bytes=VMEM_LIMIT)


def _resident(shape):
    return pl.BlockSpec(shape, lambda *_: (0,) * len(shape), pipeline_mode=pl.Buffered(1))


def _rows(width, tile=ROW_TILE):
    return pl.BlockSpec((tile, width), lambda i: (i, 0))


PROMPT_TILES = N_PROMPT // ROW_TILE


def _split_rows(width):
    return [pl.BlockSpec((ROW_TILE, width), lambda i: (jnp.minimum(i, PROMPT_TILES - 1), 0)),
            pl.BlockSpec((ROW_TILE, width), lambda i: (jnp.maximum(i - PROMPT_TILES, 0), 0))]


def _split_shapes(width):
    return [jax.ShapeDtypeStruct((N_PROMPT, width), F32), jax.ShapeDtypeStruct((N_SAMPLE, width), F32)]


def _read_split(prompt_ref, sample_ref):
    return jnp.where(pl.program_id(0) < PROMPT_TILES, prompt_ref[...], sample_ref[...])


def _for_split(prompt_ref, sample_ref, body):
    step = pl.program_id(0)

    @pl.when(step < PROMPT_TILES)
    def _():
        body(prompt_ref)

    @pl.when(step >= PROMPT_TILES)
    def _():
        body(sample_ref)


def _rms(x, g):
    return x * lax.rsqrt(jnp.mean(x * x, axis=-1, keepdims=True) + RMS_EPS) * g


def _gmlp_in_kernel(xp_ref, xs_ref, g_ref, w_ref, lng_ref, lnb_ref, u_ref, v_ref):
    xn = _rms(_read_split(xp_ref, xs_ref), g_ref[...]).astype(BF16)
    u_ref[...] = jax.nn.gelu(jnp.dot(xn, w_ref[:, :A_WIDTH], preferred_element_type=F32))
    v = jax.nn.gelu(jnp.dot(xn, w_ref[:, A_WIDTH:], preferred_element_type=F32))
    mu = jnp.mean(v, axis=-1, keepdims=True)
    vc = v - mu
    var = jnp.mean(vc * vc, axis=-1, keepdims=True)
    v_ref[...] = vc * lax.rsqrt(var + LN_EPS) * lng_ref[...] + lnb_ref[...]


def _gmlp_in(x_prompt, x_sample, g, w_in, ln_g, ln_b):
    return pl.pallas_call(
        _gmlp_in_kernel,
        grid=(N_TOK // ROW_TILE,),
        in_specs=_split_rows(D_MODEL) + [_resident((1, D_MODEL)), _resident((D_MODEL, 2 * A_WIDTH)),
                                         _resident((1, A_WIDTH)), _resident((1, A_WIDTH))],
        out_specs=[_rows(A_WIDTH), _rows(A_WIDTH)],
        out_shape=[jax.ShapeDtypeStruct((N_TOK, A_WIDTH), F32)] * 2,
        compiler_params=_params("parallel"),
    )(x_prompt, x_sample, g, w_in, ln_g, ln_b)


def _gmlp_out_kernel(u_ref, v_ref, xp_ref, xs_ref, wmix_ref, bias_ref, wout_ref, o_ref, um_ref):
    for c in range(ROW_TILE // CHUNK):
        rows = slice(c * CHUNK, (c + 1) * CHUNK)
        for g in range(A_GROUPS):
            cols = slice(g * A_GROUP_DIM, (g + 1) * A_GROUP_DIM)
            mixed = jnp.dot(wmix_ref[0, g], v_ref[rows, cols].astype(BF16),
                            preferred_element_type=F32) + bias_ref[0, :, cols]
            um_ref[rows, cols] = (u_ref[rows, cols] * mixed).astype(BF16)
    o_ref[...] = _read_split(xp_ref, xs_ref) + jnp.dot(um_ref[...], wout_ref[...], preferred_element_type=F32)


def _gmlp_out(u, v, x_prompt, x_sample, wmix, bias, w_out):
    return pl.pallas_call(
        _gmlp_out_kernel,
        grid=(N_TOK // ROW_TILE,),
        in_specs=[_rows(A_WIDTH), _rows(A_WIDTH)] + _split_rows(D_MODEL)
        + [pl.BlockSpec((1, A_GROUPS, CHUNK, CHUNK), lambda i: (i // PROMPT_TILES, 0, 0, 0)),
           pl.BlockSpec((1, CHUNK, A_WIDTH), lambda i: (i // PROMPT_TILES, 0, 0)),
           _resident((A_WIDTH, D_MODEL))],
        out_specs=_rows(D_MODEL),
        out_shape=jax.ShapeDtypeStruct((N_TOK, D_MODEL), F32),
        scratch_shapes=[pltpu.VMEM((ROW_TILE, A_WIDTH), BF16)],
        compiler_params=_params("parallel"),
    )(u, v, x_prompt, x_sample, wmix, bias, w_out)


PEER_NTOP = PEER_TOPK + 1
TOP_SECOND = 24
TOP_ROWS = 48


SUBLANES = 8
SORT16 = ((0, 1), (2, 3), (0, 2), (1, 3), (1, 2), (4, 5), (6, 7), (4, 6), (5, 7), (5, 6), (0, 4), (2, 6), (2, 4),
          (1, 5), (3, 7), (3, 5), (1, 2), (3, 4), (5, 6), (8, 9), (10, 11), (8, 10), (9, 11), (9, 10), (12, 13),
          (14, 15), (12, 14), (13, 15), (13, 14), (8, 12), (10, 14), (10, 12), (9, 13), (11, 15), (11, 13),
          (9, 10), (11, 12), (13, 14), (0, 8), (4, 12), (4, 8), (2, 10), (6, 14), (6, 10), (2, 4), (6, 8),
          (10, 12), (1, 9), (5, 13), (5, 9), (3, 11), (7, 15), (7, 11), (3, 5), (7, 9), (11, 13), (1, 2), (3, 4),
          (5, 6), (7, 8), (9, 10), (11, 12), (13, 14))


def _top_values(s, top_ref, base):
    lists = [s[SUBLANES * j:SUBLANES * (j + 1), :] for j in range(N_KEYS // SUBLANES)]
    for i, j in SORT16:
        lists[i], lists[j] = jnp.maximum(lists[i], lists[j]), jnp.minimum(lists[i], lists[j])
    for p in range(PEER_NTOP):
        m = jnp.max(lists[0], axis=0, keepdims=True)
        top_ref[base + p:base + p + 1, :] = m
        popped = lists[0] == m
        for d in range(min(len(lists), PEER_NTOP - 1 - p)):
            below = lists[d + 1] if d + 1 < len(lists) else -jnp.inf
            lists[d] = jnp.where(popped, below, lists[d])


def _rank_head(s1, s2, top_ref):
    _top_values(s1, top_ref, 0)
    _top_values(s2, top_ref, TOP_SECOND)

    def first(lo, hi):
        return top_ref[lo:hi, :]

    def second(lo, hi):
        return top_ref[TOP_SECOND + lo:TOP_SECOND + hi, :]

    m1, m2 = first(0, 1), second(0, 1)
    cands = [first(0, 8) + second(b, b + 1) for b in range(3)]
    cands += [first(8, 16) + m2, first(16, 24) + m2]
    cands += [m1 + second(0, 8), m1 + second(8, 16), m1 + second(16, 24)]
    cands += [first(a, a + 1) + second(0, 8) for a in range(1, 4)]
    top = m1 + m2
    zsum = jnp.zeros_like(top)
    ranked = []
    for k in range(PEER_NTOP):
        m = jnp.max(functools.reduce(jnp.maximum, cands), axis=0, keepdims=True)
        ranked.append(m)
        if k < PEER_TOPK:
            zsum = zsum + jnp.exp(m - top)
        if k + 1 < PEER_NTOP:
            cands = [jnp.where(c == m, -jnp.inf, c) for c in cands]
    thr = 0.5 * (ranked[PEER_TOPK - 1] + ranked[PEER_TOPK])
    return jnp.exp((thr - m2) - s1), jnp.exp(s1 - m1) / zsum, jnp.exp(s2 - m2)


def _peer_route_kernel(h_ref, g_ref, wpq_ref, keys_ref, xnt_ref, c1_ref, g1_ref, g2_ref, top_ref):
    xn_f32 = _rms(h_ref[...], g_ref[...])
    xn = xn_f32.astype(BF16)
    xnt_ref[...] = xn_f32.T.astype(BF16)
    q = jnp.dot(xn, wpq_ref[...], preferred_element_type=F32).astype(BF16)
    top_ref[...] = jnp.full(top_ref.shape, -jnp.inf, F32)
    for hd in range(PEER_HEADS):
        c0 = hd * 2 * PEER_HALF
        s1 = lax.dot_general(keys_ref[hd, 0], q[:, c0:c0 + PEER_HALF], NT_DIMS,
                             preferred_element_type=F32)
        s2 = lax.dot_general(keys_ref[hd, 1], q[:, c0 + PEER_HALF:c0 + 2 * PEER_HALF], NT_DIMS,
                             preferred_element_type=F32)
        c1_ref[hd], g1_ref[hd], g2_ref[hd] = _rank_head(s1, s2, top_ref)


def _peer_route(h, g, w_pq, keys):
    route_spec = pl.BlockSpec((PEER_HEADS, N_KEYS, ROW_TILE), lambda i: (0, 0, i))
    route_shape = jax.ShapeDtypeStruct((PEER_HEADS, N_KEYS, N_TOK), F32)
    return pl.pallas_call(
        _peer_route_kernel,
        grid=(N_TOK // ROW_TILE,),
        in_specs=[_rows(D_MODEL), _resident((1, D_MODEL)), _resident((D_MODEL, D_MODEL)),
                  _resident((PEER_HEADS, 2, N_KEYS, PEER_HALF))],
        out_specs=[pl.BlockSpec((D_MODEL, ROW_TILE), lambda i: (0, i))] + [route_spec] * 3,
        out_shape=[jax.ShapeDtypeStruct((D_MODEL, N_TOK), BF16)] + [route_shape] * 3,
        scratch_shapes=[pltpu.VMEM((TOP_ROWS, ROW_TILE), F32)],
        compiler_params=_params("parallel"),
    )(h, g, w_pq, keys)


PEER_TOK_STEPS = N_TOK // PEER_TOK_TILE
PEER_EXP_STEPS = N_EXPERTS // PEER_EXP_TILE
PEER_STEPS = PEER_TOK_STEPS * PEER_EXP_STEPS
PEER_LAG = 2
PEER_SUB_KEYS = 16
PEER_CHUNKS = 2


def _peer_dense_kernel(xt_ref, u_ref, vt_ref, c1_ref, g1_ref, g2_ref, y_ref,
                       acc_ref, act_a_ref, act_b_ref, w_a_ref, w_b_ref):
    s = pl.program_id(0)
    contract_exp_step = jnp.maximum(s - PEER_LAG, 0) % PEER_EXP_STEPS

    @pl.when(s == 0)
    def _():
        act_b_ref[...] = jnp.zeros_like(act_b_ref)
        w_a_ref[...] = jnp.zeros_like(w_a_ref)

    @pl.when(contract_exp_step == 0)
    def _():
        acc_ref[...] = jnp.zeros_like(acc_ref)

    def step(act_new_ref, act_old_ref, w_new_ref, w_old_ref):
        act_rows = PEER_EXP_TILE // PEER_CHUNKS
        acc_rows = D_MODEL // PEER_CHUNKS
        first_keys = PEER_I1_PER_STEP // PEER_CHUNKS

        def first_key_row(ref, hd, k, ls, i):
            rows = [ref[hd, c * first_keys + k:c * first_keys + k + 1, ls] for c in range(PEER_CHUNKS)]
            row = rows[-1]
            for c in range(PEER_CHUNKS - 2, -1, -1):
                row = jnp.where(i == c, rows[c], row)
            return row

        def chunk(i, carry):
            a0 = pl.multiple_of(i * act_rows, act_rows)
            c0 = pl.multiple_of(i * acc_rows, acc_rows)
            act_new_ref[pl.ds(a0, act_rows), :] = jnp.dot(
                u_ref[0, pl.ds(a0, act_rows), :], xt_ref[...], preferred_element_type=F32)
            acc_ref[pl.ds(c0, acc_rows), :] += jnp.dot(vt_ref[0, 0, pl.ds(c0, acc_rows), :], w_old_ref[...],
                                                       preferred_element_type=F32)
            for k in range(first_keys):
                for lt in range(PEER_TOK_TILE // LANES):
                    ls = slice(lt * LANES, (lt + 1) * LANES)
                    c1 = [first_key_row(c1_ref, hd, k, ls, i) for hd in range(PEER_HEADS)]
                    g1 = [first_key_row(g1_ref, hd, k, ls, i) for hd in range(PEER_HEADS)]
                    for k0 in range(0, N_KEYS, PEER_SUB_KEYS):
                        ks = slice(k0, k0 + PEER_SUB_KEYS)
                        rs = pl.ds(pl.multiple_of((i * first_keys + k) * N_KEYS + k0, PEER_SUB_KEYS),
                                   PEER_SUB_KEYS)
                        coef = None
                        for hd in range(PEER_HEADS):
                            g2 = g2_ref[hd, ks, ls]
                            term = jnp.where(g2 >= c1[hd], g2, 0.0) * g1[hd]
                            coef = term if coef is None else coef + term
                        w_new_ref[rs, ls] = (jax.nn.gelu(act_old_ref[rs, ls]) * coef).astype(BF16)
            return carry

        lax.fori_loop(0, PEER_CHUNKS, chunk, 0)

    @pl.when(s % 2 == 0)
    def _():
        step(act_a_ref, act_b_ref, w_b_ref, w_a_ref)

    @pl.when(s % 2 == 1)
    def _():
        step(act_b_ref, act_a_ref, w_a_ref, w_b_ref)

    @pl.when((contract_exp_step == PEER_EXP_STEPS - 1) & (s >= PEER_LAG))
    def _():
        y_ref[...] = acc_ref[...].T


def _peer_dense(xnt, u_bf, vt_bf, layer, c1, g1, g2):
    def tile(s, lag):
        return jnp.clip(s - lag, 0, PEER_STEPS - 1)

    def tok(s, lag):
        return tile(s, lag) // PEER_EXP_STEPS

    def exp(s, lag):
        return tile(s, lag) % PEER_EXP_STEPS

    first_spec = pl.BlockSpec((PEER_HEADS, PEER_I1_PER_STEP, PEER_TOK_TILE), lambda s: (0, exp(s, 1), tok(s, 1)))
    act_shape = pltpu.VMEM((PEER_EXP_TILE, PEER_TOK_TILE), F32)
    w_shape = pltpu.VMEM((PEER_EXP_TILE, PEER_TOK_TILE), BF16)
    return pl.pallas_call(
        _peer_dense_kernel,
        grid=(PEER_STEPS + PEER_LAG,),
        in_specs=[pl.BlockSpec((D_MODEL, PEER_TOK_TILE), lambda s: (0, tok(s, 0))),
                  pl.BlockSpec((1, PEER_EXP_TILE, D_MODEL), lambda s: (layer, exp(s, 0), 0)),
                  pl.BlockSpec((1, 1, D_MODEL, PEER_EXP_TILE), lambda s: (layer, exp(s, PEER_LAG), 0, 0)),
                  first_spec, first_spec,
                  pl.BlockSpec((PEER_HEADS, N_KEYS, PEER_TOK_TILE), lambda s: (0, 0, tok(s, 1)))],
        out_specs=pl.BlockSpec((PEER_TOK_TILE, D_MODEL), lambda s: (tok(s, PEER_LAG), 0)),
        out_shape=jax.ShapeDtypeStruct((N_TOK, D_MODEL), F32),
        scratch_shapes=[pltpu.VMEM((D_MODEL, PEER_TOK_TILE), F32), act_shape, act_shape, w_shape, w_shape],
        compiler_params=_params("arbitrary"),
    )(xnt, u_bf, vt_bf, c1, g1, g2)


TABLE_TILE = 512


def _peer_tables_kernel(u_ref, v_ref, ub_ref, vt_ref):
    ub_ref[0] = u_ref[0].astype(BF16)
    vt_ref[0, 0] = v_ref[0].T.astype(BF16)


def _peer_tables(peer_u, peer_v):
    n_layers = peer_u.shape[0]
    per_tile = PEER_EXP_TILE // TABLE_TILE
    rows = pl.BlockSpec((1, TABLE_TILE, D_MODEL), lambda l, j: (l, j, 0))
    return pl.pallas_call(
        _peer_tables_kernel,
        grid=(n_layers, N_EXPERTS // TABLE_TILE),
        in_specs=[rows, rows],
        out_specs=[rows, pl.BlockSpec((1, 1, D_MODEL, TABLE_TILE), lambda l, j: (l, j // per_tile, 0, j % per_tile))],
        out_shape=[jax.ShapeDtypeStruct((n_layers, N_EXPERTS, D_MODEL), BF16),
                   jax.ShapeDtypeStruct((n_layers, PEER_EXP_STEPS, D_MODEL, PEER_EXP_TILE), BF16)],
        compiler_params=_params("parallel", "parallel"),
    )(peer_u, peer_v)


def _peer(h, g, w_pq, keys, u_bf, vt_bf, layer):
    xnt, c1, g1, g2 = _peer_route(h, g, w_pq.astype(BF16), keys.astype(BF16))
    return _peer_dense(xnt, u_bf, vt_bf, layer, c1, g1, g2)


def _ple_kernel(h_ref, y_ref, p_ref, g_ref, wg_ref, wp_ref, *rest, final):
    h = h_ref[...] + y_ref[...]
    gate = jax.nn.sigmoid(jnp.dot(_rms(h, g_ref[...]).astype(BF16), wg_ref[...], preferred_element_type=F32))
    out = h + gate * jnp.dot(p_ref[...].astype(BF16), wp_ref[...], preferred_element_type=F32)
    if final:
        gf_ref, yp_ref, ys_ref = rest
        y_final = _rms(out, gf_ref[...])

        def store(ref):
            ref[...] = y_final

        _for_split(yp_ref, ys_ref, store)
    else:
        (o_ref,) = rest
        o_ref[...] = out


def _ple(h, y, p, g, w_gate, w_proj, g_final=None):
    final = g_final is not None
    in_specs = [_rows(D_MODEL), _rows(D_MODEL), _rows(PLE_DIM), _resident((1, D_MODEL)),
                _resident((D_MODEL, D_MODEL)), _resident((PLE_DIM, D_MODEL))]
    args = [h, y, p, g, w_gate, w_proj]
    out_specs, out_shape = [_rows(D_MODEL)], [jax.ShapeDtypeStruct((N_TOK, D_MODEL), F32)]
    if final:
        in_specs.append(_resident((1, D_MODEL)))
        args.append(g_final)
        out_specs, out_shape = _split_rows(D_MODEL), _split_shapes(D_MODEL)
    return pl.pallas_call(
        functools.partial(_ple_kernel, final=final),
        grid=(N_TOK // ROW_TILE,),
        in_specs=in_specs,
        out_specs=out_specs,
        out_shape=out_shape,
        compiler_params=_params("arbitrary" if final else "parallel"),
    )(*args)


def _proj_rope_kernel(h_ref, g_ref, w_ref, cos_ref, sin_ref, op_ref, os_ref, *, n_rope_heads, width):
    def body(o_ref):
        xn = _rms(h_ref[...], g_ref[...]).astype(BF16)
        cos = cos_ref[...]
        sin = sin_ref[...]
        for c0 in range(0, width, KV_WIDTH):
            z = jnp.dot(xn, w_ref[:, c0:c0 + KV_WIDTH], preferred_element_type=F32)
            for hd in range(N_KV_HEADS):
                cs = slice(hd * HEAD_DIM, (hd + 1) * HEAD_DIM)
                zh = z[:, cs]
                if c0 // HEAD_DIM + hd < n_rope_heads:
                    zh = zh * cos + pltpu.roll(zh, HEAD_DIM // 2, 1) * sin
                o_ref[:, c0 + hd * HEAD_DIM:c0 + (hd + 1) * HEAD_DIM] = zh

    _for_split(op_ref, os_ref, body)


def _proj_rope(h, g, w, cos, sin, n_rope_heads):
    width = w.shape[1]
    return pl.pallas_call(
        functools.partial(_proj_rope_kernel, n_rope_heads=n_rope_heads, width=width),
        grid=(N_TOK // ROW_TILE,),
        in_specs=[_rows(D_MODEL), _resident((1, D_MODEL)), _resident((D_MODEL, width)),
                  _rows(HEAD_DIM), _rows(HEAD_DIM)],
        out_specs=_split_rows(width),
        out_shape=_split_shapes(width),
        compiler_params=_params("arbitrary"),
    )(h, g, w, cos, sin)


def _softmax_parts(scores):
    m = functools.reduce(jnp.maximum, [jnp.max(s, axis=-1, keepdims=True) for s in scores])
    probs = [jnp.exp(s - m) for s in scores]
    z = functools.reduce(jnp.add, [jnp.sum(p, axis=-1, keepdims=True) for p in probs])
    return probs, z, m + jnp.log(z)


def _attn_prompt_kernel(q0_ref, q1_ref, q2_ref, k_ref, v_ref, o_ref, og_ref, lse_ref):
    iq = lax.broadcasted_iota(jnp.int32, (ATTN_BLOCK, 2 * ATTN_BLOCK), 0)
    ik = lax.broadcasted_iota(jnp.int32, (ATTN_BLOCK, 2 * ATTN_BLOCK), 1)
    diff = ATTN_BLOCK + iq - ik
    two_blocks = (diff >= 0) & (diff <= ATTN_BLOCK)
    first_pair = ik <= iq
    own_block = (lax.broadcasted_iota(jnp.int32, (ATTN_BLOCK, ATTN_BLOCK), 0)
                 >= lax.broadcasted_iota(jnp.int32, (ATTN_BLOCK, ATTN_BLOCK), 1))
    for g, (q_ref, (_, dil)) in enumerate(zip((q0_ref, q1_ref, q2_ref), DILATED_GROUPS)):
        n_blocks = SEQ // dil // ATTN_BLOCK
        items = []
        for res in range(dil):
            for blk in range(n_blocks):
                first = blk * ATTN_BLOCK * dil + res
                rows = pl.ds(first, ATTN_BLOCK, stride=dil)
                if n_blocks == 1:
                    items.append((rows, rows, own_block))
                elif blk == 0:
                    items.append((rows, pl.ds(first, 2 * ATTN_BLOCK, stride=dil), first_pair))
                else:
                    items.append((rows, pl.ds(first - ATTN_BLOCK * dil, 2 * ATTN_BLOCK, stride=dil), two_blocks))
        for i0 in range(0, len(items), ATTN_BATCH):
            batch = items[i0:i0 + ATTN_BATCH]
            q = jnp.stack([q_ref[r, :] for r, _, _ in batch]).astype(BF16)
            k = jnp.stack([k_ref[kk, :] for _, kk, _ in batch]).astype(BF16)
            v = jnp.stack([v_ref[kk, :] for _, kk, _ in batch]).astype(BF16)
            valid = jnp.stack([m for _, _, m in batch])
            s = jnp.einsum('bqd,bkd->bqk', q, k, preferred_element_type=F32) * ATTN_SCALE
            (p,), z, lse = _softmax_parts([jnp.where(valid, s, -jnp.inf)])
            o = jnp.einsum('bqk,bkd->bqd', p.astype(BF16), v, preferred_element_type=F32) / z
            for b, (r, _, _) in enumerate(batch):
                og_ref[g, r, :] = o[b]
                lse_ref[g, r, :] = jnp.broadcast_to(lse[b], (ATTN_BLOCK, HEAD_DIM))
    for row0 in range(0, SEQ, 2 * ATTN_BLOCK):
        rs = slice(row0, row0 + 2 * ATTN_BLOCK)
        lses = [lse_ref[g, rs, :] for g in range(N_DGROUPS)]
        top = functools.reduce(jnp.maximum, lses)
        wts = [jnp.exp(l - top) for l in lses]
        merged = functools.reduce(jnp.add, [w * og_ref[g, rs, :] for g, w in enumerate(wts)])
        o_ref[rs, :] = merged / functools.reduce(jnp.add, wts)


def _attn_prompt(q, kv):
    blk = (SEQ, HEAD_DIM)
    group_scratch = pltpu.VMEM((N_DGROUPS, SEQ, HEAD_DIM), F32)
    return pl.pallas_call(
        _attn_prompt_kernel,
        grid=(BATCH, N_KV_HEADS),
        in_specs=[pl.BlockSpec(blk, lambda b, h, g=g: (b, g * N_KV_HEADS + h)) for g in range(N_DGROUPS)]
        + [pl.BlockSpec(blk, lambda b, h: (b, h)), pl.BlockSpec(blk, lambda b, h: (b, N_KV_HEADS + h))],
        out_specs=pl.BlockSpec(blk, lambda b, h: (b, h)),
        out_shape=jax.ShapeDtypeStruct((N_PROMPT, KV_WIDTH), F32),
        scratch_shapes=[group_scratch, group_scratch],
        compiler_params=_params("parallel", "parallel"),
    )(q, q, q, kv, kv)


CACHE_PAGE = 16
CACHE_PAGES = PAST_LEN // CACHE_PAGE
G1_PAGES = DILATED_GROUPS[1][0] // CACHE_PAGE
G0_PAGES = DILATED_GROUPS[0][0] // CACHE_PAGE


G2_OLD_PAGES = CACHE_PAGES - G1_PAGES
QH = DEC_SEQ * N_KV_HEADS


def _sample_masks():
    row = jnp.arange(QH)[:, None]
    j, h = row // N_KV_HEADS, row % N_KV_HEADS

    def mask(n_pos, ok):
        col = jnp.arange(n_pos * N_KV_HEADS)[None, :]
        pos, hk = col // N_KV_HEADS, col % N_KV_HEADS
        return jnp.where((hk == h) & ok(pos), 0.0, -jnp.inf).astype(F32)

    n0, n1 = G0_PAGES * CACHE_PAGE, G1_PAGES * CACHE_PAGE
    past = [mask(n0, lambda c: c >= j),
            mask(n1, lambda c: (c >= j) & ((n1 + j - c) % DILATED_GROUPS[1][1] == 0)),
            mask(CACHE_PAGES * DEC_SEQ, lambda p: p % DEC_SEQ == j)]
    new = [mask(DEC_SEQ, lambda c: c <= j),
           mask(DEC_SEQ, lambda c: (c <= j) & ((j - c) % DILATED_GROUPS[1][1] == 0)),
           mask(DEC_SEQ, lambda c: c == j)]
    return past, jnp.stack(new)


def _attn_sample_kernel(q_ref, kvn_ref, ca_ref, cb_ref, m0_ref, m1_ref, m2_ref, mnew_ref, o_ref):
    def keys_values(ref, *idx):
        return tuple(ref[idx + (kv,)].reshape(-1, HEAD_DIM).astype(BF16) for kv in range(2))

    n_old = G2_OLD_PAGES * DEC_SEQ * N_KV_HEADS
    new = keys_values(kvn_ref, 0, slice(None))
    parts = [
        [(keys_values(cb_ref, 0, slice(G1_PAGES - G0_PAGES, None), slice(None)), m0_ref[...])],
        [(keys_values(cb_ref, 0, slice(None), slice(None)), m1_ref[...])],
        [(keys_values(ca_ref, 0, slice(None), slice(None)), m2_ref[:, :n_old]),
         (keys_values(cb_ref, 0, slice(None), slice(0, DEC_SEQ)), m2_ref[:, n_old:])],
    ]
    outs, lses = [], []
    for g in range(N_DGROUPS):
        q = q_ref[0, g].reshape(QH, HEAD_DIM).astype(BF16)
        group = parts[g] + [(new, mnew_ref[g])]
        scores = [lax.dot_general(q, k, NT_DIMS, preferred_element_type=F32) * ATTN_SCALE + m
                  for (k, _), m in group]
        probs, z, lse = _softmax_parts(scores)
        o = functools.reduce(jnp.add, [jnp.dot(p.astype(BF16), v, preferred_element_type=F32)
                                       for p, ((_, v), _) in zip(probs, group)])
        outs.append(o / z)
        lses.append(lse)
    top = functools.reduce(jnp.maximum, lses)
    wts = [jnp.exp(l - top) for l in lses]
    merged = functools.reduce(jnp.add, [w * o for w, o in zip(wts, outs)])
    o_ref[0] = (merged / functools.reduce(jnp.add, wts)).reshape(DEC_SEQ, N_KV_HEADS, HEAD_DIM)


def _attn_sample(q_sample, kv_sample, cache_kv):
    q5 = q_sample.reshape(DEC_BATCH, DEC_SEQ, N_DGROUPS, N_KV_HEADS, HEAD_DIM).transpose(0, 2, 1, 3, 4)
    cache = cache_kv.reshape(DEC_BATCH, CACHE_PAGES, CACHE_PAGE, 2, N_KV_HEADS, HEAD_DIM)
    (m0, m1, m2), m_new = _sample_masks()
    tile_dims = (2, N_KV_HEADS, HEAD_DIM)
    o = pl.pallas_call(
        _attn_sample_kernel,
        grid=(DEC_BATCH,),
        in_specs=[pl.BlockSpec((1, N_DGROUPS, DEC_SEQ, N_KV_HEADS, HEAD_DIM), lambda b: (b, 0, 0, 0, 0)),
                  pl.BlockSpec((1, DEC_SEQ) + tile_dims, lambda b: (b, 0, 0, 0, 0)),
                  pl.BlockSpec((1, G2_OLD_PAGES, DEC_SEQ) + tile_dims, lambda b: (b, 0, 0, 0, 0, 0)),
                  pl.BlockSpec((1, G1_PAGES, CACHE_PAGE) + tile_dims,
                               lambda b: (b, CACHE_PAGES // G1_PAGES - 1, 0, 0, 0, 0)),
                  _resident(m0.shape), _resident(m1.shape), _resident(m2.shape), _resident(m_new.shape)],
        out_specs=pl.BlockSpec((1, DEC_SEQ, N_KV_HEADS, HEAD_DIM), lambda b: (b, 0, 0, 0)),
        out_shape=jax.ShapeDtypeStruct((DEC_BATCH, DEC_SEQ, N_KV_HEADS, HEAD_DIM), F32),
        compiler_params=_params("parallel"),
    )(q5, kv_sample, cache, cache, m0, m1, m2, m_new)
    return o.reshape(N_SAMPLE, KV_WIDTH)


def _attn_out_kernel(op_ref, os_ref, h_ref, wo_ref, out_ref):
    o = _read_split(op_ref, os_ref).astype(BF16)
    out_ref[...] = h_ref[...] + jnp.dot(o, wo_ref[...], preferred_element_type=F32)


def _attn_out(o_prompt, o_sample, h, w_o):
    return pl.pallas_call(
        _attn_out_kernel,
        grid=(N_TOK // ROW_TILE,),
        in_specs=_split_rows(KV_WIDTH) + [_rows(D_MODEL), _resident((KV_WIDTH, D_MODEL))],
        out_specs=_rows(D_MODEL),
        out_shape=jax.ShapeDtypeStruct((N_TOK, D_MODEL), F32),
        compiler_params=_params("parallel"),
    )(o_prompt, o_sample, h, w_o)


def _rope_tables():
    half = HEAD_DIM // 2
    inv_freq = ROPE_THETA ** (-jnp.arange(half, dtype=F32) / half)
    pos = jnp.concatenate([jnp.tile(jnp.arange(SEQ, dtype=jnp.int32), BATCH),
                           jnp.tile(PAST_LEN + jnp.arange(DEC_SEQ, dtype=jnp.int32), DEC_BATCH)])
    ang = pos.astype(F32)[:, None] * inv_freq[None, :]
    cos, sin = jnp.cos(ang), jnp.sin(ang)
    return jnp.concatenate([cos, cos], axis=-1), jnp.concatenate([-sin, sin], axis=-1)


def _mix_weights(w_s, b_s):
    tril = jnp.tril(jnp.ones((CHUNK, CHUNK), dtype=bool))
    w_prompt = jnp.where(tril, w_s, 0.0)
    reps = CHUNK // DEC_SEQ
    small = w_prompt[:, :DEC_SEQ, :DEC_SEQ]
    eye = jnp.eye(reps, dtype=F32)
    w_sample = jnp.einsum('ab,gts->gatbs', eye, small).reshape(A_GROUPS, CHUNK, CHUNK)
    wmix = jnp.stack([w_prompt, w_sample]).astype(BF16)
    b_prompt = b_s.T
    b_sample = jnp.tile(b_s[:, :DEC_SEQ].T, (reps, 1))
    bias = jnp.repeat(jnp.stack([b_prompt, b_sample]), A_GROUP_DIM, axis=-1)
    return wmix, bias


def kernel(x_prompt, x_sample, p_prompt, p_sample, cache_kv, g_mix, a_w_in, a_ln_g, a_ln_b, a_w_s, a_b_s,
           a_w_out, g_kv, w_kv, b_w_q, b_w_o, g_ffn, peer_w_q, peer_keys, peer_u, peer_v, g_ple,
           ple_w_gate, ple_w_proj, g_final):
    xp, xs = x_prompt.reshape(N_PROMPT, D_MODEL), x_sample.reshape(N_SAMPLE, D_MODEL)
    p = jnp.concatenate([p_prompt.reshape(2, N_PROMPT, PLE_DIM), p_sample.reshape(2, N_SAMPLE, PLE_DIM)], axis=1)
    cos, sin = _rope_tables()
    u_bf, vt_bf = _peer_tables(peer_u, peer_v)
    row = lambda a: a.reshape(1, -1)

    u, v = _gmlp_in(xp, xs, row(g_mix[0]), a_w_in[0].astype(BF16), row(a_ln_g[0]), row(a_ln_b[0]))
    wmix, bias = _mix_weights(a_w_s[0], a_b_s[0])
    h = _gmlp_out(u, v, xp, xs, wmix, bias, a_w_out[0].astype(BF16))
    y = _peer(h, row(g_ffn[0]), peer_w_q[0], peer_keys[0], u_bf, vt_bf, 0)
    (h,) = _ple(h, y, p[0], row(g_ple[0]), ple_w_gate[0].astype(BF16), ple_w_proj[0].astype(BF16))

    kv_prompt, kv_sample = _proj_rope(h, row(g_kv), w_kv.astype(BF16), cos, sin, N_KV_HEADS)
    q_prompt, q_sample = _proj_rope(h, row(g_mix[1]), b_w_q[0].astype(BF16), cos, sin, N_DGROUPS * N_KV_HEADS)

    kv_sample = kv_sample.reshape(DEC_BATCH, DEC_SEQ, 2, N_KV_HEADS, HEAD_DIM)
    o_sample = _attn_sample(q_sample.reshape(DEC_BATCH, DEC_SEQ, Q_WIDTH), kv_sample, cache_kv)
    h = _attn_out(_attn_prompt(q_prompt, kv_prompt), o_sample, h, b_w_o[0].astype(BF16))
    y = _peer(h, row(g_ffn[1]), peer_w_q[1], peer_keys[1], u_bf, vt_bf, 1)
    y_prompt, y_sample = _ple(h, y, p[1], row(g_ple[1]), ple_w_gate[1].astype(BF16),
                              ple_w_proj[1].astype(BF16), row(g_final))

    y_prompt = y_prompt.reshape(BATCH, SEQ, D_MODEL)
    y_sample = y_sample.reshape(DEC_BATCH, DEC_SEQ, D_MODEL)
    a_v_prompt = jnp.stack([v[(b + 1) * SEQ - CHUNK:(b + 1) * SEQ] for b in range(BATCH)])[None]
    a_v_sample = v[N_PROMPT:].reshape(1, DEC_BATCH, DEC_SEQ, A_WIDTH)
    kv_prompt = kv_prompt.reshape(BATCH, SEQ, 2, N_KV_HEADS, HEAD_DIM)
    return (y_prompt, y_sample, a_v_prompt, a_v_sample, kv_prompt, kv_sample)
```

```python
import functools

import jax
import jax.numpy as jnp
from jax import lax
from jax.experimental import pallas as pl
from jax.experimental.pallas import tpu as pltpu

F32 = jnp.float32
BF16 = jnp.bfloat16

D_MODEL = 2048
BATCH = 4
SEQ = 2048
DEC_BATCH = 128
DEC_SEQ = 8
PAST_LEN = 2048
N_PROMPT = BATCH * SEQ
N_SAMPLE = DEC_BATCH * DEC_SEQ
N_TOK = N_PROMPT + N_SAMPLE

CHUNK = 128
A_WIDTH = D_MODEL
A_GROUPS = 8
A_GROUP_DIM = A_WIDTH // A_GROUPS

HEAD_DIM = 128
N_KV_HEADS = 8
KV_WIDTH = N_KV_HEADS * HEAD_DIM
DILATED_GROUPS = ((128, 1), (512, 4), (2048, 16))
N_DGROUPS = len(DILATED_GROUPS)
Q_WIDTH = N_DGROUPS * KV_WIDTH
ATTN_BLOCK = 128
ATTN_BATCH = 8
ROPE_THETA = 10000.0
ATTN_SCALE = HEAD_DIM ** -0.5

N_KEYS = 128
N_EXPERTS = N_KEYS * N_KEYS
PEER_HEADS = 8
PEER_HALF = 128
PEER_TOPK = 16
PLE_DIM = 256
RMS_EPS = 1e-6
LN_EPS = 1e-5

V7X_VMEM_BYTES = 64 * 1024 * 1024
VMEM_LIMIT = V7X_VMEM_BYTES - 8 * 1024 * 1024
LANES = 128

ROW_TILE = 256
PEER_TOK_TILE = 512
PEER_EXP_TILE = 1024
PEER_I1_PER_STEP = PEER_EXP_TILE // N_KEYS

NT_DIMS = (((1,), (1,)), ((), ()))


def _params(*sem):
    return pltpu.CompilerParams(dimension_semantics=sem, vmem_limit_bytes=VMEM_LIMIT)


def _resident(shape):
    return pl.BlockSpec(shape, lambda *_: (0,) * len(shape), pipeline_mode=pl.Buffered(1))


def _rows(width, tile=ROW_TILE):
    return pl.BlockSpec((tile, width), lambda i: (i, 0))


PROMPT_TILES = N_PROMPT // ROW_TILE


def _split_rows(width):
    return [pl.BlockSpec((ROW_TILE, width), lambda i: (jnp.minimum(i, PROMPT_TILES - 1), 0)),
            pl.BlockSpec((ROW_TILE, width), lambda i: (jnp.maximum(i - PROMPT_TILES, 0), 0))]


def _split_shapes(width):
    return [jax.ShapeDtypeStruct((N_PROMPT, width), F32), jax.ShapeDtypeStruct((N_SAMPLE, width), F32)]


def _read_split(prompt_ref, sample_ref):
    return jnp.where(pl.program_id(0) < PROMPT_TILES, prompt_ref[...], sample_ref[...])


def _for_split(prompt_ref, sample_ref, body):
    step = pl.program_id(0)

    @pl.when(step < PROMPT_TILES)
    def _():
        body(prompt_ref)

    @pl.when(step >= PROMPT_TILES)
    def _():
        body(sample_ref)


def _rms(x, g):
    return x * lax.rsqrt(jnp.mean(x * x, axis=-1, keepdims=True) + RMS_EPS) * g


def _gmlp_in_kernel(xp_ref, xs_ref, g_ref, w_ref, lng_ref, lnb_ref, u_ref, v_ref):
    xn = _rms(_read_split(xp_ref, xs_ref), g_ref[...]).astype(BF16)
    u_ref[...] = jax.nn.gelu(jnp.dot(xn, w_ref[:, :A_WIDTH], preferred_element_type=F32))
    v = jax.nn.gelu(jnp.dot(xn, w_ref[:, A_WIDTH:], preferred_element_type=F32))
    mu = jnp.mean(v, axis=-1, keepdims=True)
    vc = v - mu
    var = jnp.mean(vc * vc, axis=-1, keepdims=True)
    v_ref[...] = vc * lax.rsqrt(var + LN_EPS) * lng_ref[...] + lnb_ref[...]


def _gmlp_in(x_prompt, x_sample, g, w_in, ln_g, ln_b):
    return pl.pallas_call(
        _gmlp_in_kernel,
        grid=(N_TOK // ROW_TILE,),
        in_specs=_split_rows(D_MODEL) + [_resident((1, D_MODEL)), _resident((D_MODEL, 2 * A_WIDTH)),
                                         _resident((1, A_WIDTH)), _resident((1, A_WIDTH))],
        out_specs=[_rows(A_WIDTH), _rows(A_WIDTH)],
        out_shape=[jax.ShapeDtypeStruct((N_TOK, A_WIDTH), F32)] * 2,
        compiler_params=_params("parallel"),
    )(x_prompt, x_sample, g, w_in, ln_g, ln_b)


def _gmlp_out_kernel(u_ref, v_ref, xp_ref, xs_ref, wmix_ref, bias_ref, wout_ref, o_ref, um_ref):
    for c in range(ROW_TILE // CHUNK):
        rows = slice(c * CHUNK, (c + 1) * CHUNK)
        for g in range(A_GROUPS):
            cols = slice(g * A_GROUP_DIM, (g + 1) * A_GROUP_DIM)
            mixed = jnp.dot(wmix_ref[0, g], v_ref[rows, cols].astype(BF16),
                            preferred_element_type=F32) + bias_ref[0, :, cols]
            um_ref[rows, cols] = (u_ref[rows, cols] * mixed).astype(BF16)
    o_ref[...] = _read_split(xp_ref, xs_ref) + jnp.dot(um_ref[...], wout_ref[...], preferred_element_type=F32)


def _gmlp_out(u, v, x_prompt, x_sample, wmix, bias, w_out):
    return pl.pallas_call(
        _gmlp_out_kernel,
        grid=(N_TOK // ROW_TILE,),
        in_specs=[_rows(A_WIDTH), _rows(A_WIDTH)] + _split_rows(D_MODEL)
        + [pl.BlockSpec((1, A_GROUPS, CHUNK, CHUNK), lambda i: (i // PROMPT_TILES, 0, 0, 0)),
           pl.BlockSpec((1, CHUNK, A_WIDTH), lambda i: (i // PROMPT_TILES, 0, 0)),
           _resident((A_WIDTH, D_MODEL))],
        out_specs=_rows(D_MODEL),
        out_shape=jax.ShapeDtypeStruct((N_TOK, D_MODEL), F32),
        scratch_shapes=[pltpu.VMEM((ROW_TILE, A_WIDTH), BF16)],
        compiler_params=_params("parallel"),
    )(u, v, x_prompt, x_sample, wmix, bias, w_out)


PEER_NTOP = PEER_TOPK + 1
TOP_SECOND = 24
TOP_ROWS = 48


SUBLANES = 8
SORT16 = ((0, 1), (2, 3), (0, 2), (1, 3), (1, 2), (4, 5), (6, 7), (4, 6), (5, 7), (5, 6), (0, 4), (2, 6), (2, 4),
          (1, 5), (3, 7), (3, 5), (1, 2), (3, 4), (5, 6), (8, 9), (10, 11), (8, 10), (9, 11), (9, 10), (12, 13),
          (14, 15), (12, 14), (13, 15), (13, 14), (8, 12), (10, 14), (10, 12), (9, 13), (11, 15), (11, 13),
          (9, 10), (11, 12), (13, 14), (0, 8), (4, 12), (4, 8), (2, 10), (6, 14), (6, 10), (2, 4), (6, 8),
          (10, 12), (1, 9), (5, 13), (5, 9), (3, 11), (7, 15), (7, 11), (3, 5), (7, 9), (11, 13), (1, 2), (3, 4),
          (5, 6), (7, 8), (9, 10), (11, 12), (13, 14))


def _top_values(s, top_ref, base):
    lists = [s[SUBLANES * j:SUBLANES * (j + 1), :] for j in range(N_KEYS // SUBLANES)]
    for i, j in SORT16:
        lists[i], lists[j] = jnp.maximum(lists[i], lists[j]), jnp.minimum(lists[i], lists[j])
    for p in range(PEER_NTOP):
        m = jnp.max(lists[0], axis=0, keepdims=True)
        top_ref[base + p:base + p + 1, :] = m
        popped = lists[0] == m
        for d in range(min(len(lists), PEER_NTOP - 1 - p)):
            below = lists[d + 1] if d + 1 < len(lists) else -jnp.inf
            lists[d] = jnp.where(popped, below, lists[d])


def _rank_head(s1, s2, top_ref):
    _top_values(s1, top_ref, 0)
    _top_values(s2, top_ref, TOP_SECOND)

    def first(lo, hi):
        return top_ref[lo:hi, :]

    def second(lo, hi):
        return top_ref[TOP_SECOND + lo:TOP_SECOND + hi, :]

    m1, m2 = first(0, 1), second(0, 1)
    cands = [first(0, 8) + second(b, b + 1) for b in range(3)]
    cands += [first(8, 16) + m2, first(16, 24) + m2]
    cands += [m1 + second(0, 8), m1 + second(8, 16), m1 + second(16, 24)]
    cands += [first(a, a + 1) + second(0, 8) for a in range(1, 4)]
    top = m1 + m2
    zsum = jnp.zeros_like(top)
    ranked = []
    for k in range(PEER_NTOP):
        m = jnp.max(functools.reduce(jnp.maximum, cands), axis=0, keepdims=True)
        ranked.append(m)
        if k < PEER_TOPK:
            zsum = zsum + jnp.exp(m - top)
        if k + 1 < PEER_NTOP:
            cands = [jnp.where(c == m, -jnp.inf, c) for c in cands]
    thr = 0.5 * (ranked[PEER_TOPK - 1] + ranked[PEER_TOPK])
    return jnp.exp((thr - m2) - s1), jnp.exp(s1 - m1) / zsum, jnp.exp(s2 - m2)


def _peer_route_kernel(h_ref, g_ref, wpq_ref, keys_ref, xnt_ref, c1_ref, g1_ref, g2_ref, top_ref):
    xn_f32 = _rms(h_ref[...], g_ref[...])
    xn = xn_f32.astype(BF16)
    xnt_ref[...] = xn_f32.T.astype(BF16)
    q = jnp.dot(xn, wpq_ref[...], preferred_element_type=F32).astype(BF16)
    top_ref[...] = jnp.full(top_ref.shape, -jnp.inf, F32)
    for hd in range(PEER_HEADS):
        c0 = hd * 2 * PEER_HALF
        s1 = lax.dot_general(keys_ref[hd, 0], q[:, c0:c0 + PEER_HALF], NT_DIMS,
                             preferred_element_type=F32)
        s2 = lax.dot_general(keys_ref[hd, 1], q[:, c0 + PEER_HALF:c0 + 2 * PEER_HALF], NT_DIMS,
                             preferred_element_type=F32)
        c1_ref[hd], g1_ref[hd], g2_ref[hd] = _rank_head(s1, s2, top_ref)


def _peer_route(h, g, w_pq, keys):
    route_spec = pl.BlockSpec((PEER_HEADS, N_KEYS, ROW_TILE), lambda i: (0, 0, i))
    route_shape = jax.ShapeDtypeStruct((PEER_HEADS, N_KEYS, N_TOK), F32)
    return pl.pallas_call(
        _peer_route_kernel,
        grid=(N_TOK // ROW_TILE,),
        in_specs=[_rows(D_MODEL), _resident((1, D_MODEL)), _resident((D_MODEL, D_MODEL)),
                  _resident((PEER_HEADS, 2, N_KEYS, PEER_HALF))],
        out_specs=[pl.BlockSpec((D_MODEL, ROW_TILE), lambda i: (0, i))] + [route_spec] * 3,
        out_shape=[jax.ShapeDtypeStruct((D_MODEL, N_TOK), BF16)] + [route_shape] * 3,
        scratch_shapes=[pltpu.VMEM((TOP_ROWS, ROW_TILE), F32)],
        compiler_params=_params("parallel"),
    )(h, g, w_pq, keys)


PEER_TOK_STEPS = N_TOK // PEER_TOK_TILE
PEER_EXP_STEPS = N_EXPERTS // PEER_EXP_TILE
PEER_STEPS = PEER_TOK_STEPS * PEER_EXP_STEPS
PEER_LAG = 2
PEER_SUB_KEYS = 16
PEER_CHUNKS = 2


def _peer_dense_kernel(xt_ref, u_ref, vt_ref, c1_ref, g1_ref, g2_ref, y_ref,
                       acc_ref, act_a_ref, act_b_ref, w_a_ref, w_b_ref):
    s = pl.program_id(0)
    contract_exp_step = jnp.maximum(s - PEER_LAG, 0) % PEER_EXP_STEPS

    @pl.when(s == 0)
    def _():
        act_b_ref[...] = jnp.zeros_like(act_b_ref)
        w_a_ref[...] = jnp.zeros_like(w_a_ref)

    @pl.when(contract_exp_step == 0)
    def _():
        acc_ref[...] = jnp.zeros_like(acc_ref)

    def step(act_new_ref, act_old_ref, w_new_ref, w_old_ref):
        act_rows = PEER_EXP_TILE // PEER_CHUNKS
        acc_rows = D_MODEL // PEER_CHUNKS
        first_keys = PEER_I1_PER_STEP // PEER_CHUNKS

        def first_key_row(ref, hd, k, ls, i):
            rows = [ref[hd, c * first_keys + k:c * first_keys + k + 1, ls] for c in range(PEER_CHUNKS)]
            row = rows[-1]
            for c in range(PEER_CHUNKS - 2, -1, -1):
                row = jnp.where(i == c, rows[c], row)
            return row

        def chunk(i, carry):
            a0 = pl.multiple_of(i * act_rows, act_rows)
            c0 = pl.multiple_of(i * acc_rows, acc_rows)
            act_new_ref[pl.ds(a0, act_rows), :] = jnp.dot(
                u_ref[0, pl.ds(a0, act_rows), :], xt_ref[...], preferred_element_type=F32)
            acc_ref[pl.ds(c0, acc_rows), :] += jnp.dot(vt_ref[0, 0, pl.ds(c0, acc_rows), :], w_old_ref[...],
                                                       preferred_element_type=F32)
            for k in range(first_keys):
                for lt in range(PEER_TOK_TILE // LANES):
                    ls = slice(lt * LANES, (lt + 1) * LANES)
                    c1 = [first_key_row(c1_ref, hd, k, ls, i) for hd in range(PEER_HEADS)]
                    g1 = [first_key_row(g1_ref, hd, k, ls, i) for hd in range(PEER_HEADS)]
                    for k0 in range(0, N_KEYS, PEER_SUB_KEYS):
                        ks = slice(k0, k0 + PEER_SUB_KEYS)
                        rs = pl.ds(pl.multiple_of((i * first_keys + k) * N_KEYS + k0, PEER_SUB_KEYS),
                                   PEER_SUB_KEYS)
                        coef = None
                        for hd in range(PEER_HEADS):
                            g2 = g2_ref[hd, ks, ls]
                            term = jnp.where(g2 >= c1[hd], g2, 0.0) * g1[hd]
                            coef = term if coef is None else coef + term
                        w_new_ref[rs, ls] = (jax.nn.gelu(act_old_ref[rs, ls]) * coef).astype(BF16)
            return carry

        lax.fori_loop(0, PEER_CHUNKS, chunk, 0)

    @pl.when(s % 2 == 0)
    def _():
        step(act_a_ref, act_b_ref, w_b_ref, w_a_ref)

    @pl.when(s % 2 == 1)
    def _():
        step(act_b_ref, act_a_ref, w_a_ref, w_b_ref)

    @pl.when((contract_exp_step == PEER_EXP_STEPS - 1) & (s >= PEER_LAG))
    def _():
        y_ref[...] = acc_ref[...].T


def _peer_dense(xnt, u_bf, vt_bf, layer, c1, g1, g2):
    def tile(s, lag):
        return jnp.clip(s - lag, 0, PEER_STEPS - 1)

    def tok(s, lag):
        return tile(s, lag) // PEER_EXP_STEPS

    def exp(s, lag):
        return tile(s, lag) % PEER_EXP_STEPS

    first_spec = pl.BlockSpec((PEER_HEADS, PEER_I1_PER_STEP, PEER_TOK_TILE), lambda s: (0, exp(s, 1), tok(s, 1)))
    act_shape = pltpu.VMEM((PEER_EXP_TILE, PEER_TOK_TILE), F32)
    w_shape = pltpu.VMEM((PEER_EXP_TILE, PEER_TOK_TILE), BF16)
    return pl.pallas_call(
        _peer_dense_kernel,
        grid=(PEER_STEPS + PEER_LAG,),
        in_specs=[pl.BlockSpec((D_MODEL, PEER_TOK_TILE), lambda s: (0, tok(s, 0))),
                  pl.BlockSpec((1, PEER_EXP_TILE, D_MODEL), lambda s: (layer, exp(s, 0), 0)),
                  pl.BlockSpec((1, 1, D_MODEL, PEER_EXP_TILE), lambda s: (layer, exp(s, PEER_LAG), 0, 0)),
                  first_spec, first_spec,
                  pl.BlockSpec((PEER_HEADS, N_KEYS, PEER_TOK_TILE), lambda s: (0, 0, tok(s, 1)))],
        out_specs=pl.BlockSpec((PEER_TOK_TILE, D_MODEL), lambda s: (tok(s, PEER_LAG), 0)),
        out_shape=jax.ShapeDtypeStruct((N_TOK, D_MODEL), F32),
        scratch_shapes=[pltpu.VMEM((D_MODEL, PEER_TOK_TILE), F32), act_shape, act_shape, w_shape, w_shape],
        compiler_params=_params("arbitrary"),
    )(xnt, u_bf, vt_bf, c1, g1, g2)


TABLE_TILE = 512


def _peer_tables_kernel(u_ref, v_ref, ub_ref, vt_ref):
    ub_ref[0] = u_ref[0].astype(BF16)
    vt_ref[0, 0] = v_ref[0].T.astype(BF16)


def _peer_tables(peer_u, peer_v):
    n_layers = peer_u.shape[0]
    per_tile = PEER_EXP_TILE // TABLE_TILE
    rows = pl.BlockSpec((1, TABLE_TILE, D_MODEL), lambda l, j: (l, j, 0))
    return pl.pallas_call(
        _peer_tables_kernel,
        grid=(n_layers, N_EXPERTS // TABLE_TILE),
        in_specs=[rows, rows],
        out_specs=[rows, pl.BlockSpec((1, 1, D_MODEL, TABLE_TILE), lambda l, j: (l, j // per_tile, 0, j % per_tile))],
        out_shape=[jax.ShapeDtypeStruct((n_layers, N_EXPERTS, D_MODEL), BF16),
                   jax.ShapeDtypeStruct((n_layers, PEER_EXP_STEPS, D_MODEL, PEER_EXP_TILE), BF16)],
        compiler_params=_params("parallel", "parallel"),
    )(peer_u, peer_v)


def _peer(h, g, w_pq, keys, u_bf, vt_bf, layer):
    xnt, c1, g1, g2 = _peer_route(h, g, w_pq.astype(BF16), keys.astype(BF16))
    return _peer_dense(xnt, u_bf, vt_bf, layer, c1, g1, g2)


def _ple_kernel(h_ref, y_ref, p_ref, g_ref, wg_ref, wp_ref, *rest, final):
    h = h_ref[...] + y_ref[...]
    gate = jax.nn.sigmoid(jnp.dot(_rms(h, g_ref[...]).astype(BF16), wg_ref[...], preferred_element_type=F32))
    out = h + gate * jnp.dot(p_ref[...].astype(BF16), wp_ref[...], preferred_element_type=F32)
    if final:
        gf_ref, yp_ref, ys_ref = rest
        y_final = _rms(out, gf_ref[...])

        def store(ref):
            ref[...] = y_final

        _for_split(yp_ref, ys_ref, store)
    else:
        (o_ref,) = rest
        o_ref[...] = out


def _ple(h, y, p, g, w_gate, w_proj, g_final=None):
    final = g_final is not None
    in_specs = [_rows(D_MODEL), _rows(D_MODEL), _rows(PLE_DIM), _resident((1, D_MODEL)),
                _resident((D_MODEL, D_MODEL)), _resident((PLE_DIM, D_MODEL))]
    args = [h, y, p, g, w_gate, w_proj]
    out_specs, out_shape = [_rows(D_MODEL)], [jax.ShapeDtypeStruct((N_TOK, D_MODEL), F32)]
    if final:
        in_specs.append(_resident((1, D_MODEL)))
        args.append(g_final)
        out_specs, out_shape = _split_rows(D_MODEL), _split_shapes(D_MODEL)
    return pl.pallas_call(
        functools.partial(_ple_kernel, final=final),
        grid=(N_TOK // ROW_TILE,),
        in_specs=in_specs,
        out_specs=out_specs,
        out_shape=out_shape,
        compiler_params=_params("arbitrary" if final else "parallel"),
    )(*args)


def _proj_rope_kernel(h_ref, g_ref, w_ref, cos_ref, sin_ref, op_ref, os_ref, *, n_rope_heads, width):
    def body(o_ref):
        xn = _rms(h_ref[...], g_ref[...]).astype(BF16)
        cos = cos_ref[...]
        sin = sin_ref[...]
        for c0 in range(0, width, KV_WIDTH):
            z = jnp.dot(xn, w_ref[:, c0:c0 + KV_WIDTH], preferred_element_type=F32)
            for hd in range(N_KV_HEADS):
                cs = slice(hd * HEAD_DIM, (hd + 1) * HEAD_DIM)
                zh = z[:, cs]
                if c0 // HEAD_DIM + hd < n_rope_heads:
                    zh = zh * cos + pltpu.roll(zh, HEAD_DIM // 2, 1) * sin
                o_ref[:, c0 + hd * HEAD_DIM:c0 + (hd + 1) * HEAD_DIM] = zh

    _for_split(op_ref, os_ref, body)


def _kv_q_kernel(h_ref, gkv_ref, gq_ref, wkv_ref, wq_ref, cos_ref, sin_ref, kvp_ref, kvs_ref, qp_ref, qs_ref):
    def project(xn, w_ref, o_ref, n_rope_heads):
        cos = cos_ref[...]
        sin = sin_ref[...]
        for c0 in range(0, w_ref.shape[1], KV_WIDTH):
            z = jnp.dot(xn, w_ref[:, c0:c0 + KV_WIDTH], preferred_element_type=F32)
            for hd in range(N_KV_HEADS):
                zh = z[:, hd * HEAD_DIM:(hd + 1) * HEAD_DIM]
                if c0 // HEAD_DIM + hd < n_rope_heads:
                    zh = zh * cos + pltpu.roll(zh, HEAD_DIM // 2, 1) * sin
                o_ref[:, c0 + hd * HEAD_DIM:c0 + (hd + 1) * HEAD_DIM] = zh

    def body(kv_ref, q_ref):
        x = h_ref[...]
        xr = x * lax.rsqrt(jnp.mean(x * x, axis=-1, keepdims=True) + RMS_EPS)
        project((xr * gkv_ref[...]).astype(BF16), wkv_ref, kv_ref, N_KV_HEADS)
        project((xr * gq_ref[...]).astype(BF16), wq_ref, q_ref, N_DGROUPS * N_KV_HEADS)

    step = pl.program_id(0)

    @pl.when(step < PROMPT_TILES)
    def _():
        body(kvp_ref, qp_ref)

    @pl.when(step >= PROMPT_TILES)
    def _():
        body(kvs_ref, qs_ref)


def _kv_q(h, g_kv, g_q, w_kv, w_q, cos, sin):
    return pl.pallas_call(
        _kv_q_kernel,
        grid=(N_TOK // ROW_TILE,),
        in_specs=[_rows(D_MODEL), _resident((1, D_MODEL)), _resident((1, D_MODEL)),
                  _resident((D_MODEL, 2 * KV_WIDTH)), _resident((D_MODEL, Q_WIDTH)),
                  _rows(HEAD_DIM), _rows(HEAD_DIM)],
        out_specs=_split_rows(2 * KV_WIDTH) + _split_rows(Q_WIDTH),
        out_shape=_split_shapes(2 * KV_WIDTH) + _split_shapes(Q_WIDTH),
        compiler_params=_params("arbitrary"),
    )(h, g_kv, g_q, w_kv, w_q, cos, sin)


def _proj_rope(h, g, w, cos, sin, n_rope_heads):
    width = w.shape[1]
    return pl.pallas_call(
        functools.partial(_proj_rope_kernel, n_rope_heads=n_rope_heads, width=width),
        grid=(N_TOK // ROW_TILE,),
        in_specs=[_rows(D_MODEL), _resident((1, D_MODEL)), _resident((D_MODEL, width)),
                  _rows(HEAD_DIM), _rows(HEAD_DIM)],
        out_specs=_split_rows(width),
        out_shape=_split_shapes(width),
        compiler_params=_params("arbitrary"),
    )(h, g, w, cos, sin)


def _softmax_parts(scores):
    m = functools.reduce(jnp.maximum, [jnp.max(s, axis=-1, keepdims=True) for s in scores])
    probs = [jnp.exp(s - m) for s in scores]
    z = functools.reduce(jnp.add, [jnp.sum(p, axis=-1, keepdims=True) for p in probs])
    return probs, z, m + jnp.log(z)


def _attn_prompt_kernel(q0_ref, q1_ref, q2_ref, k_ref, v_ref, o_ref, og_ref, lse_ref):
    iq = lax.broadcasted_iota(jnp.int32, (ATTN_BLOCK, 2 * ATTN_BLOCK), 0)
    ik = lax.broadcasted_iota(jnp.int32, (ATTN_BLOCK, 2 * ATTN_BLOCK), 1)
    diff = ATTN_BLOCK + iq - ik
    two_blocks = (diff >= 0) & (diff <= ATTN_BLOCK)
    first_pair = ik <= iq
    own_block = (lax.broadcasted_iota(jnp.int32, (ATTN_BLOCK, ATTN_BLOCK), 0)
                 >= lax.broadcasted_iota(jnp.int32, (ATTN_BLOCK, ATTN_BLOCK), 1))
    for g, (q_ref, (_, dil)) in enumerate(zip((q0_ref, q1_ref, q2_ref), DILATED_GROUPS)):
        n_blocks = SEQ // dil // ATTN_BLOCK
        items = []
        for res in range(dil):
            for blk in range(n_blocks):
                first = blk * ATTN_BLOCK * dil + res
                rows = pl.ds(first, ATTN_BLOCK, stride=dil)
                if n_blocks == 1:
                    items.append((rows, rows, own_block))
                elif blk == 0:
                    items.append((rows, pl.ds(first, 2 * ATTN_BLOCK, stride=dil), first_pair))
                else:
                    items.append((rows, pl.ds(first - ATTN_BLOCK * dil, 2 * ATTN_BLOCK, stride=dil), two_blocks))
        for i0 in range(0, len(items), ATTN_BATCH):
            batch = items[i0:i0 + ATTN_BATCH]
            q = jnp.stack([q_ref[r, :] for r, _, _ in batch]).astype(BF16)
            k = jnp.stack([k_ref[kk, :] for _, kk, _ in batch]).astype(BF16)
            v = jnp.stack([v_ref[kk, :] for _, kk, _ in batch]).astype(BF16)
            valid = jnp.stack([m for _, _, m in batch])
            s = jnp.einsum('bqd,bkd->bqk', q, k, preferred_element_type=F32) * ATTN_SCALE
            (p,), z, lse = _softmax_parts([jnp.where(valid, s, -jnp.inf)])
            o = jnp.einsum('bqk,bkd->bqd', p.astype(BF16), v, preferred_element_type=F32) / z
            for b, (r, _, _) in enumerate(batch):
                og_ref[g, r, :] = o[b]
                lse_ref[g, r, :] = jnp.broadcast_to(lse[b], (ATTN_BLOCK, HEAD_DIM))
    for row0 in range(0, SEQ, 2 * ATTN_BLOCK):
        rs = slice(row0, row0 + 2 * ATTN_BLOCK)
        lses = [lse_ref[g, rs, :] for g in range(N_DGROUPS)]
        top = functools.reduce(jnp.maximum, lses)
        wts = [jnp.exp(l - top) for l in lses]
        merged = functools.reduce(jnp.add, [w * og_ref[g, rs, :] for g, w in enumerate(wts)])
        o_ref[rs, :] = merged / functools.reduce(jnp.add, wts)


def _attn_prompt(q, kv):
    blk = (SEQ, HEAD_DIM)
    group_scratch = pltpu.VMEM((N_DGROUPS, SEQ, HEAD_DIM), F32)
    return pl.pallas_call(
        _attn_prompt_kernel,
        grid=(BATCH, N_KV_HEADS),
        in_specs=[pl.BlockSpec(blk, lambda b, h, g=g: (b, g * N_KV_HEADS + h)) for g in range(N_DGROUPS)]
        + [pl.BlockSpec(blk, lambda b, h: (b, h)), pl.BlockSpec(blk, lambda b, h: (b, N_KV_HEADS + h))],
        out_specs=pl.BlockSpec(blk, lambda b, h: (b, h)),
        out_shape=jax.ShapeDtypeStruct((N_PROMPT, KV_WIDTH), F32),
        scratch_shapes=[group_scratch, group_scratch],
        compiler_params=_params("parallel", "parallel"),
    )(q, q, q, kv, kv)


CACHE_PAGE = 16
CACHE_PAGES = PAST_LEN // CACHE_PAGE
G1_PAGES = DILATED_GROUPS[1][0] // CACHE_PAGE
G0_PAGES = DILATED_GROUPS[0][0] // CACHE_PAGE


G2_OLD_PAGES = CACHE_PAGES - G1_PAGES
QH = DEC_SEQ * N_KV_HEADS


def _sample_masks():
    row = jnp.arange(QH)[:, None]
    j, h = row // N_KV_HEADS, row % N_KV_HEADS

    def mask(n_pos, ok):
        col = jnp.arange(n_pos * N_KV_HEADS)[None, :]
        pos, hk = col // N_KV_HEADS, col % N_KV_HEADS
        return jnp.where((hk == h) & ok(pos), 0.0, -jnp.inf).astype(F32)

    n0, n1 = G0_PAGES * CACHE_PAGE, G1_PAGES * CACHE_PAGE
    past = [mask(n0, lambda c: c >= j),
            mask(n1, lambda c: (c >= j) & ((n1 + j - c) % DILATED_GROUPS[1][1] == 0)),
            mask(CACHE_PAGES * DEC_SEQ, lambda p: p % DEC_SEQ == j)]
    new = [mask(DEC_SEQ, lambda c: c <= j),
           mask(DEC_SEQ, lambda c: (c <= j) & ((j - c) % DILATED_GROUPS[1][1] == 0)),
           mask(DEC_SEQ, lambda c: c == j)]
    return past, jnp.stack(new)


def _attn_sample_kernel(q_ref, kvn_ref, ca_ref, cb_ref, m0_ref, m1_ref, m2_ref, mnew_ref, o_ref):
    def keys_values(ref, *idx):
        return tuple(ref[idx + (kv,)].reshape(-1, HEAD_DIM).astype(BF16) for kv in range(2))

    n_old = G2_OLD_PAGES * DEC_SEQ * N_KV_HEADS
    new = keys_values(kvn_ref, 0, slice(None))
    parts = [
        [(keys_values(cb_ref, 0, slice(G1_PAGES - G0_PAGES, None), slice(None)), m0_ref[...])],
        [(keys_values(cb_ref, 0, slice(None), slice(None)), m1_ref[...])],
        [(keys_values(ca_ref, 0, slice(None), slice(None)), m2_ref[:, :n_old]),
         (keys_values(cb_ref, 0, slice(None), slice(0, DEC_SEQ)), m2_ref[:, n_old:])],
    ]
    outs, lses = [], []
    for g in range(N_DGROUPS):
        q = q_ref[0, g].reshape(QH, HEAD_DIM).astype(BF16)
        group = parts[g] + [(new, mnew_ref[g])]
        scores = [lax.dot_general(q, k, NT_DIMS, preferred_element_type=F32) * ATTN_SCALE + m
                  for (k, _), m in group]
        probs, z, lse = _softmax_parts(scores)
        o = functools.reduce(jnp.add, [jnp.dot(p.astype(BF16), v, preferred_element_type=F32)
                                       for p, ((_, v), _) in zip(probs, group)])
        outs.append(o / z)
        lses.append(lse)
    top = functools.reduce(jnp.maximum, lses)
    wts = [jnp.exp(l - top) for l in lses]
    merged = functools.reduce(jnp.add, [w * o for w, o in zip(wts, outs)])
    o_ref[0] = (merged / functools.reduce(jnp.add, wts)).reshape(DEC_SEQ, N_KV_HEADS, HEAD_DIM)


def _attn_sample(q_sample, kv_sample, cache_kv):
    q5 = q_sample.reshape(DEC_BATCH, DEC_SEQ, N_DGROUPS, N_KV_HEADS, HEAD_DIM).transpose(0, 2, 1, 3, 4)
    cache = cache_kv.reshape(DEC_BATCH, CACHE_PAGES, CACHE_PAGE, 2, N_KV_HEADS, HEAD_DIM)
    (m0, m1, m2), m_new = _sample_masks()
    tile_dims = (2, N_KV_HEADS, HEAD_DIM)
    o = pl.pallas_call(
        _attn_sample_kernel,
        grid=(DEC_BATCH,),
        in_specs=[pl.BlockSpec((1, N_DGROUPS, DEC_SEQ, N_KV_HEADS, HEAD_DIM), lambda b: (b, 0, 0, 0, 0)),
                  pl.BlockSpec((1, DEC_SEQ) + tile_dims, lambda b: (b, 0, 0, 0, 0)),
                  pl.BlockSpec((1, G2_OLD_PAGES, DEC_SEQ) + tile_dims, lambda b: (b, 0, 0, 0, 0, 0)),
                  pl.BlockSpec((1, G1_PAGES, CACHE_PAGE) + tile_dims,
                               lambda b: (b, CACHE_PAGES // G1_PAGES - 1, 0, 0, 0, 0)),
                  _resident(m0.shape), _resident(m1.shape), _resident(m2.shape), _resident(m_new.shape)],
        out_specs=pl.BlockSpec((1, DEC_SEQ, N_KV_HEADS, HEAD_DIM), lambda b: (b, 0, 0, 0)),
        out_shape=jax.ShapeDtypeStruct((DEC_BATCH, DEC_SEQ, N_KV_HEADS, HEAD_DIM), F32),
        compiler_params=_params("parallel"),
    )(q5, kv_sample, cache, cache, m0, m1, m2, m_new)
    return o.reshape(N_SAMPLE, KV_WIDTH)


def _attn_out_kernel(op_ref, os_ref, h_ref, wo_ref, out_ref):
    o = _read_split(op_ref, os_ref).astype(BF16)
    out_ref[...] = h_ref[...] + jnp.dot(o, wo_ref[...], preferred_element_type=F32)


def _attn_out(o_prompt, o_sample, h, w_o):
    return pl.pallas_call(
        _attn_out_kernel,
        grid=(N_TOK // ROW_TILE,),
        in_specs=_split_rows(KV_WIDTH) + [_rows(D_MODEL), _resident((KV_WIDTH, D_MODEL))],
        out_specs=_rows(D_MODEL),
        out_shape=jax.ShapeDtypeStruct((N_TOK, D_MODEL), F32),
        compiler_params=_params("parallel"),
    )(o_prompt, o_sample, h, w_o)


def _rope_tables():
    half = HEAD_DIM // 2
    inv_freq = ROPE_THETA ** (-jnp.arange(half, dtype=F32) / half)
    pos = jnp.concatenate([jnp.tile(jnp.arange(SEQ, dtype=jnp.int32), BATCH),
                           jnp.tile(PAST_LEN + jnp.arange(DEC_SEQ, dtype=jnp.int32), DEC_BATCH)])
    ang = pos.astype(F32)[:, None] * inv_freq[None, :]
    cos, sin = jnp.cos(ang), jnp.sin(ang)
    return jnp.concatenate([cos, cos], axis=-1), jnp.concatenate([-sin, sin], axis=-1)


def _mix_weights(w_s, b_s):
    tril = jnp.tril(jnp.ones((CHUNK, CHUNK), dtype=bool))
    w_prompt = jnp.where(tril, w_s, 0.0)
    reps = CHUNK // DEC_SEQ
    small = w_prompt[:, :DEC_SEQ, :DEC_SEQ]
    eye = jnp.eye(reps, dtype=F32)
    w_sample = jnp.einsum('ab,gts->gatbs', eye, small).reshape(A_GROUPS, CHUNK, CHUNK)
    wmix = jnp.stack([w_prompt, w_sample]).astype(BF16)
    b_prompt = b_s.T
    b_sample = jnp.tile(b_s[:, :DEC_SEQ].T, (reps, 1))
    bias = jnp.repeat(jnp.stack([b_prompt, b_sample]), A_GROUP_DIM, axis=-1)
    return wmix, bias


def kernel(x_prompt, x_sample, p_prompt, p_sample, cache_kv, g_mix, a_w_in, a_ln_g, a_ln_b, a_w_s, a_b_s,
           a_w_out, g_kv, w_kv, b_w_q, b_w_o, g_ffn, peer_w_q, peer_keys, peer_u, peer_v, g_ple,
           ple_w_gate, ple_w_proj, g_final):
    xp, xs = x_prompt.reshape(N_PROMPT, D_MODEL), x_sample.reshape(N_SAMPLE, D_MODEL)
    p = jnp.concatenate([p_prompt.reshape(2, N_PROMPT, PLE_DIM), p_sample.reshape(2, N_SAMPLE, PLE_DIM)], axis=1)
    cos, sin = _rope_tables()
    u_bf, vt_bf = _peer_tables(peer_u, peer_v)
    row = lambda a: a.reshape(1, -1)

    u, v = _gmlp_in(xp, xs, row(g_mix[0]), a_w_in[0].astype(BF16), row(a_ln_g[0]), row(a_ln_b[0]))
    wmix, bias = _mix_weights(a_w_s[0], a_b_s[0])
    h = _gmlp_out(u, v, xp, xs, wmix, bias, a_w_out[0].astype(BF16))
    y = _peer(h, row(g_ffn[0]), peer_w_q[0], peer_keys[0], u_bf, vt_bf, 0)
    (h,) = _ple(h, y, p[0], row(g_ple[0]), ple_w_gate[0].astype(BF16), ple_w_proj[0].astype(BF16))

    kv_prompt, kv_sample, q_prompt, q_sample = _kv_q(h, row(g_kv), row(g_mix[1]), w_kv.astype(BF16),
                                                     b_w_q[0].astype(BF16), cos, sin)

    kv_sample = kv_sample.reshape(DEC_BATCH, DEC_SEQ, 2, N_KV_HEADS, HEAD_DIM)
    o_sample = _attn_sample(q_sample.reshape(DEC_BATCH, DEC_SEQ, Q_WIDTH), kv_sample, cache_kv)
    h = _attn_out(_attn_prompt(q_prompt, kv_prompt), o_sample, h, b_w_o[0].astype(BF16))
    y = _peer(h, row(g_ffn[1]), peer_w_q[1], peer_keys[1], u_bf, vt_bf, 1)
    y_prompt, y_sample = _ple(h, y, p[1], row(g_ple[1]), ple_w_gate[1].astype(BF16),
                              ple_w_proj[1].astype(BF16), row(g_final))

    y_prompt = y_prompt.reshape(BATCH, SEQ, D_MODEL)
    y_sample = y_sample.reshape(DEC_BATCH, DEC_SEQ, D_MODEL)
    a_v_prompt = jnp.stack([v[(b + 1) * SEQ - CHUNK:(b + 1) * SEQ] for b in range(BATCH)])[None]
    a_v_sample = v[N_PROMPT:].reshape(1, DEC_BATCH, DEC_SEQ, A_WIDTH)
    kv_prompt = kv_prompt.reshape(BATCH, SEQ, 2, N_KV_HEADS, HEAD_DIM)
    return (y_prompt, y_sample, a_v_prompt, a_v_sample, kv_prompt, kv_sample)
```
